```python
import math
import jax
import jax.numpy as jnp
from jax import lax
import numpy as np

D_MODEL = 2048
BATCH = 8
SEQ = 2048
DEPTH = 2

CTX_LEN = 256
GRID_W = 64
HEAD_DIM = 128
ROPE_THETA = 10000.0
ROPE_FREQS = HEAD_DIM // 4
RMS_EPS = 1e-6
N_BRANCH = 4
BRANCH_WIDTH = D_MODEL // 2
Q_BLOCK = 128
A_HEADS = BRANCH_WIDTH // HEAD_DIM
A_KV_HEADS = A_HEADS // 4
B_HEADS = BRANCH_WIDTH // HEAD_DIM
B_KV_HEADS = B_HEADS // 4
WINDOW = 128
GDN_HEADS = BRANCH_WIDTH // 128
GDN_DK = 128
GDN_DV = 128
GDN_CHUNK = 64
CONV_W = 3
GLA_HEADS = 4
GLA_DV = BRANCH_WIDTH // GLA_HEADS
GLA_DK = GLA_DV // 2
GLA_RANK = 16
GLA_GATE_NORM = 16.0
GLA_CHUNK = 64
D_FF = 256 * ((8 * D_MODEL // 3 + 255) // 256)
N_EXPERTS = 8
TOP_K = 2
D_FF_EXPERT = 7 * D_MODEL // 2
N_DENSE = (DEPTH + 1) // 2
N_MOE = DEPTH // 2

PROJ_LAYOUT = (
    ('a_q', A_HEADS * HEAD_DIM), ('a_k', A_KV_HEADS * HEAD_DIM), ('a_v', A_KV_HEADS * HEAD_DIM),
    ('b_q', B_HEADS * HEAD_DIM), ('b_k', B_KV_HEADS * HEAD_DIM), ('b_v', B_KV_HEADS * HEAD_DIM),
    ('c_q', GDN_HEADS * GDN_DK), ('c_k', GDN_HEADS * GDN_DK), ('c_v', GDN_HEADS * GDN_DV),
    ('c_z', GDN_HEADS * GDN_DV), ('c_beta', 2 * GDN_HEADS), ('c_decay', 2 * GDN_HEADS),
    ('d_q', GLA_HEADS * GLA_DK), ('d_k', GLA_HEADS * GLA_DK), ('d_v', GLA_HEADS * GLA_DV),
    ('d_r', GLA_HEADS * GLA_DV), ('d_gate', 2 * GLA_RANK),
)
PROJ_WIDTH = sum(width for _, width in PROJ_LAYOUT)

kernel_name = 'hybrid_parallel_mixer_dit_prefix'


def rms_norm(x, gain):
    xf = x.astype(jnp.float32)
    y = xf * lax.rsqrt(jnp.mean(xf * xf, axis=-1, keepdims=True) + RMS_EPS)
    return (y * gain.astype(jnp.float32)).astype(x.dtype)


def l2_normalize(x):
    xf = x.astype(jnp.float32)
    return (xf * lax.rsqrt(jnp.sum(xf * xf, axis=-1, keepdims=True) + RMS_EPS)).astype(x.dtype)


def heads(t, n):
    return t.reshape(t.shape[:-1] + (n, t.shape[-1] // n))


def split_proj(z):
    bounds = [int(b) for b in np.cumsum([w for _, w in PROJ_LAYOUT])[:-1]]
    return dict(zip([name for name, _ in PROJ_LAYOUT], jnp.split(z, bounds, axis=-1)))


def modulation(cond, w_ada, b_ada):
    m = jax.nn.silu(cond) @ w_ada + b_ada
    return [mi[..., None, :] for mi in jnp.split(m, 6, axis=-1)]


def modulate(x, gain, shift, scale):
    return rms_norm(x, gain) * (1 + scale) + shift


def axial_rope_tables(rows, dtype):
    row = jnp.repeat(jnp.arange(rows, dtype=jnp.float32), GRID_W)
    col = jnp.tile(jnp.arange(GRID_W, dtype=jnp.float32), rows)
    inv_freq = ROPE_THETA ** (-jnp.arange(ROPE_FREQS, dtype=jnp.float32) / ROPE_FREQS)
    ang = jnp.stack([row[:, None] * inv_freq, col[:, None] * inv_freq], axis=1)[:, None]
    return jnp.cos(ang).astype(dtype), jnp.sin(ang).astype(dtype)


def apply_axial_rope(x, cos, sin):
    xr = x.reshape(x.shape[:-1] + (2, 2, ROPE_FREQS))
    x1, x2 = xr[..., 0, :], xr[..., 1, :]
    out = jnp.stack([x1 * cos - x2 * sin, x2 * cos + x1 * sin], axis=-2)
    return out.reshape(x.shape)


def softmax_attention(q, k, v, sink=None):
    B, Q, H, hd = q.shape
    KVH = k.shape[2]
    G = H // KVH
    s = jnp.einsum('bqkgd,bskd->bkgqs', q.reshape(B, Q, KVH, G, hd), k).astype(jnp.float32) * hd ** -0.5
    if sink is not None:
        sink_col = jnp.broadcast_to(sink.astype(jnp.float32).reshape(1, KVH, G, 1, 1), s.shape[:-1] + (1,))
        s = jnp.concatenate([s, sink_col], axis=-1)
    p = jax.nn.softmax(s, axis=-1)
    if sink is not None:
        p = p[..., :-1]
    o = jnp.einsum('bkgqs,bskd->bqkgd', p.astype(v.dtype), v)
    return o.reshape(B, Q, H * hd)


def global_attention(q_lat, k_all, v_all):
    B, S, H, hd = q_lat.shape
    nblk = S // Q_BLOCK
    qb = jnp.moveaxis(q_lat.reshape(B, nblk, Q_BLOCK, H, hd), 1, 0)
    ob = lax.map(lambda qi: softmax_attention(qi, k_all, v_all), qb)
    return jnp.moveaxis(ob, 0, 1).reshape(B, S, H * hd)


def window_attention(q, k, v, k_ctx, v_ctx, sink):
    B, S, H, hd = q.shape
    KVH = k.shape[2]
    G = H // KVH
    nblk = S // Q_BLOCK
    L = k_ctx.shape[1]
    pad = ((0, 0), (WINDOW, WINDOW), (0, 0), (0, 0))

    def band(a):
        ab = jnp.pad(a, pad).reshape(B, nblk + 2, Q_BLOCK, KVH, hd)
        return jnp.concatenate([ab[:, :-2], ab[:, 1:-1], ab[:, 2:]], axis=2)

    k_band, v_band = band(k), band(v)
    qb = q.reshape(B, nblk, Q_BLOCK, KVH, G, hd)
    scale = hd ** -0.5
    s_loc = jnp.einsum('bnqkgd,bnskd->bnkgqs', qb, k_band).astype(jnp.float32) * scale
    s_ctx = jnp.einsum('bnqkgd,bskd->bnkgqs', qb, k_ctx).astype(jnp.float32) * scale
    qpos = jnp.arange(nblk)[:, None] * Q_BLOCK + jnp.arange(Q_BLOCK)[None, :]
    kpos = jnp.arange(nblk)[:, None] * Q_BLOCK - WINDOW + jnp.arange(3 * Q_BLOCK)[None, :]
    valid = ((jnp.abs(qpos[:, :, None] - kpos[:, None, :]) <= WINDOW)
             & (kpos[:, None, :] >= 0) & (kpos[:, None, :] < S))
    s_loc = jnp.where(valid[None, :, None, None], s_loc, -jnp.inf)
    sink_col = jnp.broadcast_to(sink.astype(jnp.float32).reshape(1, 1, KVH, G, 1, 1), s_ctx.shape[:-1] + (1,))
    p = jax.nn.softmax(jnp.concatenate([s_ctx, s_loc, sink_col], axis=-1), axis=-1).astype(v.dtype)
    o = (jnp.einsum('bnkgqs,bskd->bnqkgd', p[..., :L], v_ctx)
         + jnp.einsum('bnkgqs,bnskd->bnqkgd', p[..., L:L + 3 * Q_BLOCK], v_band))
    return o.reshape(B, S, H * hd)


def short_conv(x, w):
    pad = CONV_W // 2
    return lax.conv_general_dilated(x, w[:, None, :].astype(x.dtype), window_strides=(1,),
                                    padding=[(pad, pad)], dimension_numbers=('NWC', 'WIO', 'NWC'),
                                    feature_group_count=x.shape[-1])


def to_chunks(a, size):
    B, T = a.shape[:2]
    a = a.reshape((B, T // size, size) + a.shape[2:])
    return jnp.swapaxes(jnp.moveaxis(a, 1, 0), 2, 3)


def from_chunks(o):
    o = jnp.moveaxis(jnp.swapaxes(o, 2, 3), 0, 1)
    return o.reshape((o.shape[0], o.shape[1] * o.shape[2]) + o.shape[3:])


def gated_delta_chunked(q, k, v, beta, g, state):
    out_dtype = v.dtype
    f32 = jnp.float32
    C = GDN_CHUNK
    dk = q.shape[-1]
    qc, kc, vc, bc, gc = (to_chunks(a.astype(f32), C) for a in (q, k, v, beta, g))
    gc = jnp.cumsum(gc, axis=-1)
    incl = jnp.tril(jnp.ones((C, C), bool))
    strict = jnp.tril(jnp.ones((C, C), bool), -1)
    decay = jnp.exp(jnp.where(incl, gc[..., :, None] - gc[..., None, :], -jnp.inf))
    kb = kc * bc[..., None]
    a_mat = jnp.where(strict, jnp.einsum('nbhid,nbhjd->nbhij', kb, kc) * decay, 0.0) + jnp.eye(C, dtype=f32)
    rhs = jnp.concatenate([kb * jnp.exp(gc)[..., None], vc * bc[..., None]], axis=-1)
    wu = lax.linalg.triangular_solve(a_mat, rhs, left_side=True, lower=True, unit_diagonal=True)
    w, u = wu[..., :dk], wu[..., dk:]
    qk = jnp.einsum('nbhid,nbhjd->nbhij', qc, kc) * decay
    q_dec = qc * jnp.exp(gc)[..., None]
    k_dec = kc * jnp.exp(gc[..., -1:] - gc)[..., None]
    g_last = jnp.exp(gc[..., -1])[..., None, None]

    def step(S, xs):
        w_i, u_i, qk_i, qd_i, kd_i, gl_i = xs
        v_new = u_i - jnp.einsum('bhcd,bhde->bhce', w_i, S)
        o_i = jnp.einsum('bhcd,bhde->bhce', qd_i, S) + jnp.einsum('bhij,bhje->bhie', qk_i, v_new)
        S = S * gl_i + jnp.einsum('bhcd,bhce->bhde', kd_i, v_new)
        return S, o_i

    state, o = lax.scan(step, state.astype(f32), (w, u, qk, q_dec, k_dec, g_last))
    return from_chunks(o).astype(out_dtype), state


def gla_chunked(q, k, v, gk, state):
    out_dtype = v.dtype
    f32 = jnp.float32
    C = GLA_CHUNK
    qc, kc, vc, gc = (to_chunks(a.astype(f32), C) for a in (q, k, v, gk))
    b = jnp.cumsum(gc, axis=-2)
    b_ref = b[..., C // 2:C // 2 + 1, :]
    incl = jnp.tril(jnp.ones((C, C), bool))
    scores = jnp.einsum('nbhid,nbhjd->nbhij', qc * jnp.exp(b - b_ref), kc * jnp.exp(b_ref - b))
    o_intra = jnp.einsum('nbhij,nbhje->nbhie', jnp.where(incl, scores, 0.0), vc)
    q_dec = qc * jnp.exp(b)
    k_dec = kc * jnp.exp(b[..., -1:, :] - b)
    d_last = jnp.exp(b[..., -1, :])[..., None]

    def step(S, xs):
        qd_i, kd_i, v_i, dl_i = xs
        o_i = jnp.einsum('bhcd,bhde->bhce', qd_i, S)
        S = S * dl_i + jnp.einsum('bhcd,bhce->bhde', kd_i, v_i)
        return S, o_i

    state, o_inter = lax.scan(step, state.astype(f32), (q_dec, k_dec, vc, d_last))
    return from_chunks(o_intra + o_inter).astype(out_dtype), state


def flip_seq(a, direction):
    return a[:, ::-1] if direction == 1 else a


def bidirectional_prefix_scan(scan_fn, ctx_inputs, lat_inputs, state0):
    (c_shared, c_dir), (l_shared, l_dir) = ctx_inputs, lat_inputs
    outs_ctx, outs_lat = [], []
    for d in range(2):
        c_args = [flip_seq(a, d) for a in c_shared] + [flip_seq(a[:, :, d], d) for a in c_dir]
        l_args = [flip_seq(a, d) for a in l_shared] + [flip_seq(a[:, :, d], d) for a in l_dir]
        o_c, s_c = scan_fn(*c_args, state0)
        o_l, _ = scan_fn(*l_args, s_c)
        outs_ctx.append(flip_seq(o_c, d))
        outs_lat.append(flip_seq(o_l, d))
    return outs_lat[0] + outs_lat[1], outs_ctx[0] + outs_ctx[1]


def gdn_prepare(p, conv_w, a_log, dt_bias):
    B, T = p['c_q'].shape[:2]
    qkv = jax.nn.silu(short_conv(jnp.concatenate([p['c_q'], p['c_k'], p['c_v']], axis=-1), conv_w))
    q, k, v = jnp.split(qkv, [GDN_HEADS * GDN_DK, 2 * GDN_HEADS * GDN_DK], axis=-1)
    q = l2_normalize(heads(q, GDN_HEADS)) * GDN_DK ** -0.5
    k = l2_normalize(heads(k, GDN_HEADS))
    v = heads(v, GDN_HEADS)
    beta = jax.nn.sigmoid(p['c_beta'].reshape(B, T, 2, GDN_HEADS))
    g = -jnp.exp(a_log) * jax.nn.softplus(p['c_decay'].reshape(B, T, 2, GDN_HEADS) + dt_bias)
    return (q, k, v), (beta, g)


def gla_prepare(p, w_gk, b_gk):
    B, T = p['d_q'].shape[:2]
    q = heads(p['d_q'], GLA_HEADS) * GLA_DK ** -0.5
    k = heads(p['d_k'], GLA_HEADS)
    v = heads(p['d_v'], GLA_HEADS)
    low = p['d_gate'].reshape(B, T, 2, GLA_RANK)
    gk = jax.nn.log_sigmoid(jnp.einsum('btnr,nrk->btnk', low, w_gk) + b_gk) / GLA_GATE_NORM
    return (q, k, v), (gk.reshape(B, T, 2, GLA_HEADS, GLA_DK),)


def gated_head_norm(o, z, gain):
    return (rms_norm(o, gain) * jax.nn.silu(heads(z, o.shape[2]))).reshape(z.shape)


def token_mixers(z_lat, z_ctx, cos, sin, a_q_gain, a_k_gain, b_sink, gdn_conv, gdn_a_log,
                 gdn_dt_bias, gdn_norm_gain, gla_w_gk, gla_b_gk, gla_norm_gain, with_ctx_out):
    pl, pc = split_proj(z_lat), split_proj(z_ctx)
    batch = z_lat.shape[0]
    qa_l = apply_axial_rope(rms_norm(heads(pl['a_q'], A_HEADS), a_q_gain), cos, sin)
    ka_l = apply_axial_rope(rms_norm(heads(pl['a_k'], A_KV_HEADS), a_k_gain), cos, sin)
    ka_c = rms_norm(heads(pc['a_k'], A_KV_HEADS), a_k_gain)
    va_c = heads(pc['a_v'], A_KV_HEADS)
    oa_l = global_attention(qa_l, jnp.concatenate([ka_c, ka_l], axis=1),
                            jnp.concatenate([va_c, heads(pl['a_v'], A_KV_HEADS)], axis=1))
    qb_l = apply_axial_rope(heads(pl['b_q'], B_HEADS), cos, sin)
    kb_l = apply_axial_rope(heads(pl['b_k'], B_KV_HEADS), cos, sin)
    kb_c, vb_c = heads(pc['b_k'], B_KV_HEADS), heads(pc['b_v'], B_KV_HEADS)
    ob_l = window_attention(qb_l, kb_l, heads(pl['b_v'], B_KV_HEADS), kb_c, vb_c, b_sink)
    gdn_state0 = jnp.zeros((batch, GDN_HEADS, GDN_DK, GDN_DV), jnp.float32)
    oc_l, oc_c = bidirectional_prefix_scan(
        gated_delta_chunked, gdn_prepare(pc, gdn_conv, gdn_a_log, gdn_dt_bias),
        gdn_prepare(pl, gdn_conv, gdn_a_log, gdn_dt_bias), gdn_state0)
    gla_state0 = jnp.zeros((batch, GLA_HEADS, GLA_DK, GLA_DV), jnp.float32)
    od_l, od_c = bidirectional_prefix_scan(
        gla_chunked, gla_prepare(pc, gla_w_gk, gla_b_gk), gla_prepare(pl, gla_w_gk, gla_b_gk), gla_state0)
    o_lat = jnp.stack([oa_l, ob_l, gated_head_norm(oc_l, pl['c_z'], gdn_norm_gain),
                       gated_head_norm(od_l, pl['d_r'], gla_norm_gain)], axis=-2)
    if not with_ctx_out:
        return o_lat, None
    oa_c = softmax_attention(rms_norm(heads(pc['a_q'], A_HEADS), a_q_gain), ka_c, va_c)
    ob_c = softmax_attention(heads(pc['b_q'], B_HEADS), kb_c, vb_c, b_sink)
    o_ctx = jnp.stack([oa_c, ob_c, gated_head_norm(oc_c, pc['c_z'], gdn_norm_gain),
                       gated_head_norm(od_c, pc['d_r'], gla_norm_gain)], axis=-2)
    return o_lat, o_ctx


def merge_branches(h, o, w_branch, w_merge, w_out):
    y = jnp.einsum('btnf,nfd->btnd', o, w_branch)
    gate = jax.nn.sigmoid(h @ w_merge).reshape(y.shape)
    return jnp.einsum('btnd,btnd->btd', gate, y) @ w_out


def swiglu(x, w1, w3, w2):
    return (jax.nn.silu(x @ w1) * (x @ w3)) @ w2


def moe_swiglu(x, w_router, w1, w3, w2):
    t = x.reshape(-1, x.shape[-1])
    logits = (t @ w_router).astype(jnp.float32)
    top_val, top_idx = lax.top_k(logits, TOP_K)
    top_w = jax.nn.softmax(top_val, axis=-1)
    combine = jnp.sum(jax.nn.one_hot(top_idx, N_EXPERTS, dtype=jnp.float32) * top_w[..., None], axis=1).astype(t.dtype)
    y = jnp.zeros_like(t)
    for e in range(N_EXPERTS):
        y = y + combine[:, e:e + 1] * swiglu(t, w1[e], w3[e], w2[e])
    return y.reshape(x.shape)


def channel_mixer(h, layer, w1_dense, w3_dense, w2_dense, w_router, w1_moe, w3_moe, w2_moe):
    if layer % 2 == 0:
        i = layer // 2
        return swiglu(h, w1_dense[i], w3_dense[i], w2_dense[i])
    i = layer // 2
    return moe_swiglu(h, w_router[i], w1_moe[i], w3_moe[i], w2_moe[i])


def setup_inputs(seed: int = 0) -> dict:
    key = jax.random.key(seed)
    ks = jax.random.split(key, 30)
    f32 = jnp.float32
    D = D_MODEL

    def nrm(k, shape, scale):
        return jax.random.normal(k, shape, f32) * scale

    def gain(k, shape):
        return 1.0 + 0.05 * jax.random.normal(k, shape, f32)

    a_log = jnp.log(jax.random.uniform(ks[13], (DEPTH, 2, GDN_HEADS), f32, minval=1.0, maxval=16.0))
    dt = jnp.exp(jax.random.uniform(ks[14], (DEPTH, 2, GDN_HEADS), f32,
                                    minval=math.log(1e-3), maxval=math.log(1e-1)))
    dt_bias = dt + jnp.log(-jnp.expm1(-dt))
    return {
        'x': nrm(ks[0], (BATCH, SEQ, D), 1.0),
        'c': nrm(ks[1], (BATCH, D), 1.0),
        'ctx': nrm(ks[2], (BATCH, CTX_LEN, D), 1.0),
        'c_ctx': nrm(ks[3], (D,), 1.0),
        'w_ada': nrm(ks[4], (DEPTH, D, 6 * D), 0.5 * D ** -0.5),
        'b_ada': nrm(ks[5], (DEPTH, 6 * D), 0.02),
        'g_norm_mix': gain(ks[6], (DEPTH, D)),
        'g_norm_ffn': gain(ks[7], (DEPTH, D)),
        'w_in': nrm(ks[8], (DEPTH, D, PROJ_WIDTH), D ** -0.5),
        'a_q_gain': gain(ks[9], (DEPTH, HEAD_DIM)),
        'a_k_gain': gain(ks[10], (DEPTH, HEAD_DIM)),
        'b_sink': nrm(ks[11], (DEPTH, B_HEADS), 1.0),
        'gdn_conv': nrm(ks[12], (DEPTH, CONV_W, 3 * GDN_HEADS * GDN_DK), CONV_W ** -0.5),
        'gdn_a_log': a_log,
        'gdn_dt_bias': dt_bias,
        'gdn_norm_gain': gain(ks[15], (DEPTH, GDN_DV)),
        'gla_w_gk': nrm(ks[16], (DEPTH, 2, GLA_RANK, GLA_HEADS * GLA_DK), GLA_RANK ** -0.5),
        'gla_b_gk': nrm(ks[17], (DEPTH, 2, GLA_HEADS * GLA_DK), 0.1),
        'gla_norm_gain': gain(ks[18], (DEPTH, GLA_DV)),
        'w_branch': nrm(ks[19], (DEPTH, N_BRANCH, BRANCH_WIDTH, D), BRANCH_WIDTH ** -0.5),
        'w_merge': nrm(ks[20], (DEPTH, D, N_BRANCH * D), D ** -0.5),
        'w_out': nrm(ks[21], (DEPTH, D, D), D ** -0.5),
        'w1_dense': nrm(ks[22], (N_DENSE, D, D_FF), D ** -0.5),
        'w3_dense': nrm(ks[23], (N_DENSE, D, D_FF), D ** -0.5),
        'w2_dense': nrm(ks[24], (N_DENSE, D_FF, D), D_FF ** -0.5),
        'w_router': nrm(ks[25], (N_MOE, D, N_EXPERTS), D ** -0.5),
        'w1_moe': nrm(ks[26], (N_MOE, N_EXPERTS, D, D_FF_EXPERT), D ** -0.5),
        'w3_moe': nrm(ks[27], (N_MOE, N_EXPERTS, D, D_FF_EXPERT), D ** -0.5),
        'w2_moe': nrm(ks[28], (N_MOE, N_EXPERTS, D_FF_EXPERT, D), D_FF_EXPERT ** -0.5),
        'g_final': gain(ks[29], (D,)),
    }


def reference(x, c, ctx, c_ctx, w_ada, b_ada, g_norm_mix, g_norm_ffn, w_in, a_q_gain, a_k_gain,
              b_sink, gdn_conv, gdn_a_log, gdn_dt_bias, gdn_norm_gain, gla_w_gk, gla_b_gk,
              gla_norm_gain, w_branch, w_merge, w_out, w1_dense, w3_dense, w2_dense, w_router,
              w1_moe, w3_moe, w2_moe, g_final):
    ROWS = x.shape[1] // GRID_W
    cos, sin = axial_rope_tables(ROWS, x.dtype)
    x_lat, x_ctx = x, ctx
    for layer in range(DEPTH):
        last = layer == DEPTH - 1
        sh1, sc1, gt1, sh2, sc2, gt2 = modulation(c, w_ada[layer], b_ada[layer])
        csh1, csc1, cgt1, csh2, csc2, cgt2 = modulation(c_ctx, w_ada[layer], b_ada[layer])
        h_lat = modulate(x_lat, g_norm_mix[layer], sh1, sc1)
        h_ctx = modulate(x_ctx, g_norm_mix[layer], csh1, csc1)
        o_lat, o_ctx = token_mixers(h_lat @ w_in[layer], h_ctx @ w_in[layer], cos, sin,
                                    a_q_gain[layer], a_k_gain[layer], b_sink[layer], gdn_conv[layer],
                                    gdn_a_log[layer], gdn_dt_bias[layer], gdn_norm_gain[layer],
                                    gla_w_gk[layer], gla_b_gk[layer], gla_norm_gain[layer], not last)
        x_lat = x_lat + gt1 * merge_branches(h_lat, o_lat, w_branch[layer], w_merge[layer], w_out[layer])
        x_lat = x_lat + gt2 * channel_mixer(modulate(x_lat, g_norm_ffn[layer], sh2, sc2), layer,
                                            w1_dense, w3_dense, w2_dense, w_router, w1_moe, w3_moe, w2_moe)
        if not last:
            x_ctx = x_ctx + cgt1 * merge_branches(h_ctx, o_ctx, w_branch[layer], w_merge[layer], w_out[layer])
            x_ctx = x_ctx + cgt2 * channel_mixer(modulate(x_ctx, g_norm_ffn[layer], csh2, csc2), layer,
                                                w1_dense, w3_dense, w2_dense, w_router, w1_moe, w3_moe, w2_moe)
    return rms_norm(x_lat, g_final)
```

```python
import functools

import jax
import jax.numpy as jnp
from jax import lax
from jax.experimental import pallas as pl
from jax.experimental.pallas import tpu as pltpu

F32 = jnp.float32
BF16 = jnp.bfloat16

D_MODEL = 2048
HEAD_DIM = 128
GRID_W = 64
ROPE_THETA = 10000.0
ROPE_FREQS = HEAD_DIM // 4
RMS_EPS = 1e-6
BRANCH_WIDTH = D_MODEL // 2
KV_GROUP = 4
ATTN_KV_HEADS = 2
WINDOW = 128
Q_BLOCK = 128
GDN_HEADS = 8
GLA_HEADS = 4
GLA_DK = 128
GLA_DV = 256
GLA_RANK = 16
GLA_GATE_NORM = 16.0
CHUNK = 64
N_EXPERTS = 8
TOP_K = 2
N_BRANCH = 4
NEG_BIG = -1e30

COL_C_BETA = 7168
COL_D_Q = 7200
COL_D_GATE = 10272
PROJ_WIDTH = 10304
Z1_WIDTH = 7168
Z2_WIDTH = 3072

VMEM_LIMIT = 56 * 1024 * 1024
LANES = 128


def _cparams(sem):
    return pltpu.CompilerParams(dimension_semantics=sem, vmem_limit_bytes=VMEM_LIMIT)


def _silu(x):
    return x * (1.0 / (1.0 + jnp.exp(-x)))


def _sigmoid(x):
    return 1.0 / (1.0 + jnp.exp(-x))


def _split3(x):
    x1 = x.astype(BF16)
    r1 = x - x1.astype(F32)
    x2 = r1.astype(BF16)
    x3 = (r1 - x2.astype(F32)).astype(BF16)
    return x1, x2, x3


def _dot(a, b):
    return jnp.dot(a, b, preferred_element_type=F32)


def _dot_nt(a, b):
    return lax.dot_general(a, b, (((1,), (1,)), ((), ())), preferred_element_type=F32)


def _dot_tn(a, b):
    return lax.dot_general(a, b, (((0,), (0,)), ((), ())), preferred_element_type=F32)


def _dot_exact_lhs(tri_bf16, x):
    x1, x2, x3 = _split3(x)
    return _dot(tri_bf16, x1) + _dot(tri_bf16, x2) + _dot(tri_bf16, x3)


def _dot_hi(a, b):
    a1 = a.astype(BF16)
    a2 = (a - a1.astype(F32)).astype(BF16)
    b1 = b.astype(BF16)
    b2 = (b - b1.astype(F32)).astype(BF16)
    return _dot(a1, b1) + _dot(a1, b2) + _dot(a2, b1)


_CAST_ROWS = 256


def _panel_kernel(eid_ref, first_ref, valid_ref, rowid_ref, *refs, n_lhs, n_w, n_ex, n_out,
                  w_rows, epilogue):
    del eid_ref, rowid_ref
    lhs = refs[:n_lhs]
    ws = refs[n_lhs:n_lhs + n_w]
    exs = refs[n_lhs + n_w:n_lhs + n_w + n_ex]
    outs = refs[n_lhs + n_w + n_ex:n_lhs + n_w + n_ex + n_out]
    wbs = refs[n_lhs + n_w + n_ex + n_out:]
    i = pl.program_id(1)

    @pl.when(first_ref[i] == 1)
    def _():
        for w_ref, wb_ref, rows in zip(ws, wbs, w_rows):
            lead = (0,) * (len(w_ref.shape) - 2)

            def cast(r, carry, w_ref=w_ref, wb_ref=wb_ref, lead=lead):
                rr = pl.multiple_of(r * _CAST_ROWS, _CAST_ROWS)
                wb_ref[pl.ds(rr, _CAST_ROWS), :] = w_ref[lead + (pl.ds(rr, _CAST_ROWS), slice(None))].astype(BF16)
                return carry

            lax.fori_loop(0, rows // _CAST_ROWS, cast, 0)

    @pl.when(valid_ref[i] == 1)
    def _():
        epilogue(lhs, wbs, exs, outs)

    @pl.when(valid_ref[i] == 0)
    def _():
        for o in outs:
            o[...] = jnp.zeros(o.shape, o.dtype)


def _panel_call(epilogue, *, grid_n, grid_m, lhs, weights, extras, outs, eid=None, first=None,
                valid=None, rowid=None):
    if eid is None:
        eid = jnp.zeros((grid_m,), jnp.int32)
    if first is None:
        first = jnp.zeros((grid_m,), jnp.int32).at[0].set(1)
    if valid is None:
        valid = jnp.ones((grid_m,), jnp.int32)
    if rowid is None:
        rowid = jnp.zeros((grid_m,), jnp.int32)
    in_arrays, in_specs = [], []
    for arr, blk, imap in list(lhs) + list(weights) + list(extras):
        in_arrays.append(arr)
        in_specs.append(pl.BlockSpec(blk, imap))
    out_shapes = [o[0] for o in outs]
    out_specs = [pl.BlockSpec(o[1], o[2]) for o in outs]
    w_rows = [blk[-2] for _, blk, _ in weights]
    scratch = [pltpu.VMEM((blk[-2], blk[-1]), BF16) for _, blk, _ in weights]
    kern = functools.partial(_panel_kernel, n_lhs=len(lhs), n_w=len(weights), n_ex=len(extras),
                             n_out=len(outs), w_rows=w_rows, epilogue=epilogue)
    res = pl.pallas_call(
        kern,
        grid_spec=pltpu.PrefetchScalarGridSpec(
            num_scalar_prefetch=4, grid=(grid_n, grid_m),
            in_specs=in_specs, out_specs=out_specs, scratch_shapes=scratch),
        out_shape=out_shapes,
        compiler_params=_cparams(("arbitrary", "arbitrary")),
    )(eid, first, valid, rowid, *in_arrays)
    return res


def _ep_plain(lhs, wbs, exs, outs):
    outs[0][...] = _dot(lhs[0][...], wbs[0][...]).astype(outs[0].dtype)


def _ep_ada(lhs, wbs, exs, outs):
    x = _silu(lhs[0][...]).astype(BF16)
    outs[0][...] = _dot(x, wbs[0][...]) + exs[0][0]


def _ep_swiglu(lhs, wbs, exs, outs):
    x = lhs[0][...]
    a1 = _dot(x, wbs[0][...])
    a3 = _dot(x, wbs[1][...])
    outs[0][...] = (_silu(a1) * a3).astype(outs[0].dtype)


def _ep_resid(lhs, wbs, exs, outs):
    outs[0][...] = exs[0][...] + exs[1][0] * _dot(lhs[0][...], wbs[0][...])


def _ep_rowscale(lhs, wbs, exs, outs):
    outs[0][...] = exs[0][...] * _dot(lhs[0][...], wbs[0][...])


def _ep_merge(lhs, wbs, exs, outs):
    h = lhs[0][...]
    acc = None
    for n in range(N_BRANCH):
        gate = _sigmoid(_dot(h, wbs[n][...]))
        y = _dot(lhs[1 + n][...], wbs[N_BRANCH + n][...])
        acc = gate * y if acc is None else acc + gate * y
    outs[0][...] = acc.astype(outs[0].dtype)


def _matmul_stream(x, w, layer, col0, n_cols, *, tm, tn, out_dtype, grid_m=None):
    M, K = x.shape
    grid_m = M // tm if grid_m is None else grid_m
    jb = col0 // tn
    if w.ndim == 3:
        wspec = (w, (1, K, tn), lambda j, i, e, f, v, r: (layer, 0, jb + j))
    else:
        wspec = (w, (K, tn), lambda j, i, e, f, v, r: (0, jb + j))
    (out,) = _panel_call(
        _ep_plain, grid_n=n_cols // tn, grid_m=grid_m,
        lhs=[(x, (tm, K), lambda j, i, e, f, v, r: (i, 0))],
        weights=[wspec], extras=[],
        outs=[(jax.ShapeDtypeStruct((grid_m * tm, n_cols), out_dtype), (tm, tn),
               lambda j, i, e, f, v, r: (i, j))])
    return out


def _modulate_kernel(rowid_ref, x_ref, g_ref, sh_ref, sc_ref, *rest, with_router):
    del rowid_ref
    x = x_ref[...]
    y = x * lax.rsqrt(jnp.mean(x * x, axis=-1, keepdims=True) + RMS_EPS)
    h = (y * g_ref[0]) * (1.0 + sc_ref[0]) + sh_ref[0]
    if with_router:
        wr_ref, h_ref, lg_ref = rest
        h_ref[...] = h.astype(BF16)
        lg_ref[...] = _dot_hi(h, wr_ref[...])
    else:
        (h_ref,) = rest
        h_ref[...] = h.astype(BF16)


def _modulate(X, gain, mod3, rowid, layer, k_shift, k_scale, *, tm, grid_m, w_router=None):
    D = X.shape[1]
    with_router = w_router is not None
    in_specs = [
        pl.BlockSpec((tm, D), lambda i, r: (i, 0)),
        pl.BlockSpec((1, 1, D), lambda i, r: (layer, 0, 0)),
        pl.BlockSpec((1, 1, D), lambda i, r: (layer * 16 + r[i], 0, k_shift)),
        pl.BlockSpec((1, 1, D), lambda i, r: (layer * 16 + r[i], 0, k_scale)),
    ]
    args = [X, gain.reshape(gain.shape[0], 1, D), mod3, mod3]
    out_shape = [jax.ShapeDtypeStruct((grid_m * tm, D), BF16)]
    out_specs = [pl.BlockSpec((tm, D), lambda i, r: (i, 0))]
    if with_router:
        in_specs.append(pl.BlockSpec((D, LANES), lambda i, r: (0, 0)))
        args.append(w_router)
        out_shape.append(jax.ShapeDtypeStruct((grid_m * tm, LANES), F32))
        out_specs.append(pl.BlockSpec((tm, LANES), lambda i, r: (i, 0)))
    res = pl.pallas_call(
        functools.partial(_modulate_kernel, with_router=with_router),
        grid_spec=pltpu.PrefetchScalarGridSpec(
            num_scalar_prefetch=1, grid=(grid_m,), in_specs=in_specs, out_specs=out_specs),
        out_shape=out_shape,
        compiler_params=_cparams(("arbitrary",)),
    )(rowid, *args)
    return res if with_router else res[0]


def _swap_halves(x):
    lane = lax.broadcasted_iota(jnp.int32, x.shape, x.ndim - 1)
    return jnp.where((lane % 64) < 32, pltpu.roll(x, 96, x.ndim - 1), pltpu.roll(x, 32, x.ndim - 1))


def _rope(x, cos, sin_signed):
    return x * cos + _swap_halves(x) * sin_signed


def _head_rms(x, gain_row):
    return x * lax.rsqrt(jnp.mean(x * x, axis=-1, keepdims=True) + RMS_EPS) * gain_row


def _attn_kernel(*refs, mode, norm, sink, tq, S, L):
    refs = list(refs)
    sink_ref = refs.pop(0) if sink else None
    q_ref, kc_ref, vc_ref = refs[:3]
    refs = refs[3:]
    if mode != 'ctx':
        kl_ref, vl_ref, cos_ref, sin_ref = refs[:4]
        refs = refs[4:]
    if norm:
        qg_ref, kg_ref = refs[:2]
        refs = refs[2:]
    o_ref, qs_ref = refs[:2]
    kp_ref = refs[2] if len(refs) > 2 else None
    g = pl.program_id(1)
    t = pl.program_id(2)
    rows = 256

    if kp_ref is not None:
        @pl.when(t == 0)
        def _():
            if mode != 'band':
                for r0 in range(0, L, rows):
                    kc = kc_ref[r0:r0 + rows, :].astype(F32)
                    if norm:
                        kc = _head_rms(kc, kg_ref[...])
                    kp_ref[r0:r0 + rows, :] = kc.astype(BF16)
            if mode != 'ctx':
                base = L if mode == 'all' else 0
                for r0 in range(0, S, rows):
                    kl = kl_ref[r0:r0 + rows, :].astype(F32)
                    if norm:
                        kl = _head_rms(kl, kg_ref[...])
                    kl = _rope(kl, cos_ref[r0:r0 + rows, :], sin_ref[r0:r0 + rows, :])
                    kp_ref[base + r0:base + r0 + rows, :] = kl.astype(BF16)

    scale = HEAD_DIM ** -0.5
    if mode != 'ctx':
        t0 = pl.multiple_of(t * tq, tq)
        cos_q = cos_ref[pl.ds(t0, tq), :]
        sin_q = sin_ref[pl.ds(t0, tq), :]
    for j in range(KV_GROUP):
        qj = q_ref[:, j * HEAD_DIM:(j + 1) * HEAD_DIM].astype(F32)
        if norm:
            qj = _head_rms(qj, qg_ref[...])
        if mode != 'ctx':
            qj = _rope(qj, cos_q, sin_q)
        qs_ref[j * tq:(j + 1) * tq, :] = (qj * scale).astype(BF16)
    q = qs_ref[...]
    nrow = KV_GROUP * tq

    if sink:
        row = lax.broadcasted_iota(jnp.int32, (nrow, 1), 0)
        sink_col = jnp.zeros((nrow, 1), F32)
        for j in range(KV_GROUP):
            sink_col = jnp.where((row >= j * tq) & (row < (j + 1) * tq), sink_ref[g * KV_GROUP + j], sink_col)

    if mode == 'all':
        s = _dot_nt(q, kp_ref[...])
        m = jnp.max(s, axis=-1, keepdims=True)
        p = jnp.exp(s - m)
        l = jnp.sum(p, axis=-1, keepdims=True)
        pb = p.astype(BF16)
        o = _dot(pb[:, :L], vc_ref[...]) + _dot(pb[:, L:], vl_ref[...])
    elif mode == 'band':
        nb = 3 * Q_BLOCK
        start = pl.multiple_of(jnp.clip(t * tq - WINDOW, 0, S - nb), Q_BLOCK)
        s_ctx = _dot_nt(q, kc_ref[...])
        s_loc = _dot_nt(q, kp_ref[pl.ds(start, nb), :])
        qpos = t * tq + (lax.broadcasted_iota(jnp.int32, (nrow, nb), 0) % tq)
        kpos = start + lax.broadcasted_iota(jnp.int32, (nrow, nb), 1)
        s_loc = jnp.where(jnp.abs(qpos - kpos) <= WINDOW, s_loc, NEG_BIG)
        m = jnp.maximum(jnp.maximum(jnp.max(s_ctx, axis=-1, keepdims=True),
                                    jnp.max(s_loc, axis=-1, keepdims=True)), sink_col)
        p_ctx = jnp.exp(s_ctx - m)
        p_loc = jnp.exp(s_loc - m)
        l = (jnp.sum(p_ctx, axis=-1, keepdims=True) + jnp.sum(p_loc, axis=-1, keepdims=True)
             + jnp.exp(sink_col - m))
        o = (_dot(p_ctx.astype(BF16), vc_ref[...])
             + _dot(p_loc.astype(BF16), vl_ref[pl.ds(start, nb), :]))
    else:
        kc = kp_ref[...] if norm else kc_ref[...]
        s = _dot_nt(q, kc)
        m = jnp.max(s, axis=-1, keepdims=True)
        if sink:
            m = jnp.maximum(m, sink_col)
        p = jnp.exp(s - m)
        l = jnp.sum(p, axis=-1, keepdims=True)
        if sink:
            l = l + jnp.exp(sink_col - m)
        o = _dot(p.astype(BF16), vc_ref[...])
    o = o / l
    for j in range(KV_GROUP):
        o_ref[:, j * HEAD_DIM:(j + 1) * HEAD_DIM] = o[j * tq:(j + 1) * tq, :].astype(o_ref.dtype)


def _attention(z1, cos, sin, q_gain, k_gain, sink_vec, *, mode, B, S, L, qcol, kcol, vcol):
    norm = q_gain is not None
    sink = sink_vec is not None
    lat0 = 0
    ctx0 = (B * S) // L
    if mode == 'ctx':
        tq, nt = L, 1
        q_map = lambda b, g, t: (ctx0 + b, qcol + g)
    else:
        tq = Q_BLOCK
        nt = S // tq
        q_map = lambda b, g, t: (lat0 + b * nt + t, qcol + g)
    qw = KV_GROUP * HEAD_DIM
    args, in_specs = [], []
    if sink:
        args.append(sink_vec)
        in_specs.append(pl.BlockSpec(memory_space=pltpu.SMEM))
    args += [z1, z1, z1]
    in_specs += [
        pl.BlockSpec((tq, qw), q_map),
        pl.BlockSpec((L, HEAD_DIM), lambda b, g, t: (ctx0 + b, kcol + g)),
        pl.BlockSpec((L, HEAD_DIM), lambda b, g, t: (ctx0 + b, vcol + g)),
    ]
    if mode != 'ctx':
        args += [z1, z1, cos, sin]
        in_specs += [
            pl.BlockSpec((S, HEAD_DIM), lambda b, g, t: (b, kcol + g)),
            pl.BlockSpec((S, HEAD_DIM), lambda b, g, t: (b, vcol + g)),
            pl.BlockSpec((S, HEAD_DIM), lambda b, g, t: (0, 0)),
            pl.BlockSpec((S, HEAD_DIM), lambda b, g, t: (0, 0)),
        ]
    if norm:
        args += [q_gain.reshape(1, HEAD_DIM), k_gain.reshape(1, HEAD_DIM)]
        in_specs += [pl.BlockSpec((1, HEAD_DIM), lambda b, g, t: (0, 0))] * 2
    scratch = [pltpu.VMEM((KV_GROUP * tq, HEAD_DIM), BF16)]
    if mode == 'all':
        scratch.append(pltpu.VMEM((L + S, HEAD_DIM), BF16))
    elif mode == 'band':
        scratch.append(pltpu.VMEM((S, HEAD_DIM), BF16))
    elif norm:
        scratch.append(pltpu.VMEM((L, HEAD_DIM), BF16))
    n_rows = B * (L if mode == 'ctx' else S)
    if mode == 'ctx':
        o_map = lambda b, g, t: (b, g)
    else:
        o_map = lambda b, g, t: (b * nt + t, g)
    return pl.pallas_call(
        functools.partial(_attn_kernel, mode=mode, norm=norm, sink=sink, tq=tq, S=S, L=L),
        grid=(B, ATTN_KV_HEADS, nt),
        in_specs=in_specs,
        out_specs=pl.BlockSpec((tq, qw), o_map),
        out_shape=jax.ShapeDtypeStruct((n_rows, BRANCH_WIDTH), BF16),
        scratch_shapes=scratch,
        compiler_params=_cparams(("arbitrary", "arbitrary", "arbitrary")),
    )(*args)


CONV_ROWS = 256
HALO = 16


def _gdn_conv_kernel(sfirst_ref, slast_ref, prev_ref, cur_ref, next_ref, w_ref, o_ref):
    r = pl.program_id(0)
    cb = pl.program_id(1)
    x = cur_ref[...].astype(F32)
    n = x.shape[0]
    row = lax.broadcasted_iota(jnp.int32, x.shape, 0)
    prev_row = jnp.where(sfirst_ref[r] == 1, 0.0, prev_ref[...].astype(F32)[HALO - 1:HALO, :])
    next_row = jnp.where(slast_ref[r] == 1, 0.0, next_ref[...].astype(F32)[0:1, :])
    xm = jnp.where(row == 0, prev_row, pltpu.roll(x, 1, 0))
    xp = jnp.where(row == n - 1, next_row, pltpu.roll(x, n - 1, 0))
    y = _silu(w_ref[0:1, :] * xm + w_ref[1:2, :] * x + w_ref[2:3, :] * xp)
    rs = lax.rsqrt(jnp.sum(y * y, axis=-1, keepdims=True) + RMS_EPS)
    nq = GDN_HEADS
    factor = jnp.where(cb < nq, rs * (HEAD_DIM ** -0.5), jnp.where(cb < 2 * nq, rs, 1.0))
    o_ref[...] = y * factor


def _gdn_conv(z1, conv_w, seq_first, seq_last):
    M = z1.shape[0]
    nblk = M // CONV_ROWS
    ncol = 3 * GDN_HEADS
    c0 = 3072 // LANES
    per = CONV_ROWS // HALO
    nh = M // HALO
    return pl.pallas_call(
        _gdn_conv_kernel,
        grid_spec=pltpu.PrefetchScalarGridSpec(
            num_scalar_prefetch=2, grid=(nblk, ncol),
            in_specs=[
                pl.BlockSpec((HALO, LANES), lambda r, c, a, b: (jnp.maximum(r * per - 1, 0), c0 + c)),
                pl.BlockSpec((CONV_ROWS, LANES), lambda r, c, a, b: (r, c0 + c)),
                pl.BlockSpec((HALO, LANES), lambda r, c, a, b: (jnp.minimum((r + 1) * per, nh - 1), c0 + c)),
                pl.BlockSpec((3, LANES), lambda r, c, a, b: (0, c)),
            ],
            out_specs=pl.BlockSpec((CONV_ROWS, LANES), lambda r, c, a, b: (r, c))),
        out_shape=jax.ShapeDtypeStruct((M, ncol * LANES), F32),
        compiler_params=_cparams(("arbitrary", "arbitrary")),
    )(seq_first, seq_last, z1, z1, z1, conv_w)


def _chunk_tri(n, upper):
    r = lax.broadcasted_iota(jnp.int32, (n, n), 0)
    c = lax.broadcasted_iota(jnp.int32, (n, n), 1)
    same = (r // CHUNK) == (c // CHUNK)
    tri = (c >= r) if upper else (c <= r)
    return jnp.where(same & tri, 1.0, 0.0).astype(BF16)


def _gdn_gates_kernel(zg_ref, a_ref, dtb_ref, o_ref):
    zg = zg_ref[...]
    n = zg.shape[0]
    lane = lax.broadcasted_iota(jnp.int32, zg.shape, 1)
    beta = _sigmoid(zg)
    t = zg + dtb_ref[...]
    softplus = jnp.maximum(t, 0.0) + jnp.log(1.0 + jnp.exp(-jnp.abs(t)))
    g = jnp.where((lane >= 16) & (lane < 32), a_ref[...] * softplus, 0.0)
    gc_f = _dot_exact_lhs(_chunk_tri(n, False), g)
    gc_b = _dot_exact_lhs(_chunk_tri(n, True), g)
    o_ref[...] = jnp.where(lane < 16, beta, jnp.where(lane < 24, gc_f, gc_b))


def _gdn_gates(zg, a_row, dtb_row):
    M = zg.shape[0]
    tm = 256
    return pl.pallas_call(
        _gdn_gates_kernel,
        grid=(M // tm,),
        in_specs=[pl.BlockSpec((tm, LANES), lambda i: (i, 0)),
                  pl.BlockSpec((1, LANES), lambda i: (0, 0)),
                  pl.BlockSpec((1, LANES), lambda i: (0, 0))],
        out_specs=pl.BlockSpec((tm, LANES), lambda i: (i, 0)),
        out_shape=jax.ShapeDtypeStruct((M, LANES), F32),
        compiler_params=_cparams(("arbitrary",)),
    )(zg, a_row, dtb_row)


def _unit_lower_inverse(lm):
    n = lm.shape[0]
    r = lax.broadcasted_iota(jnp.int32, (n, n), 0)
    c = lax.broadcasted_iota(jnp.int32, (n, n), 1)
    eye = jnp.where(r == c, 1.0, 0.0)
    x = eye - lm
    p = lm
    k = 2
    while k < n:
        p = _dot_hi(p, p)
        x = x + _dot_hi(x, p)
        k *= 2
    return x


def _gdn_prep_kernel(q_ref, k_ref, v_ref, bf_ref, gf_ref, bb_ref, gb_ref, rf_ref, rb_ref,
                     wq_ref, u_ref, qkkd_ref, gl_ref):
    C = CHUNK
    q = q_ref[...]
    k = k_ref[...]
    v = v_ref[...]
    qb = q.astype(BF16)
    kb16 = k.astype(BF16)
    kk = _dot_nt(kb16, kb16)
    qk = _dot_nt(qb, kb16)
    r = lax.broadcasted_iota(jnp.int32, (C, C), 0)
    c = lax.broadcasted_iota(jnp.int32, (C, C), 1)
    for d, (beta_ref, gc_ref, grow_ref) in enumerate(((bf_ref, gf_ref, rf_ref), (bb_ref, gb_ref, rb_ref))):
        beta = beta_ref[0]
        gc = gc_ref[0]
        grow = grow_ref[0, 0][0:1, :]
        incl = (c <= r) if d == 0 else (c >= r)
        strict = (c < r) if d == 0 else (c > r)
        last = C - 1 if d == 0 else 0
        decay = jnp.where(incl, jnp.exp(jnp.where(incl, gc[:, 0:C] - grow, 0.0)), 0.0)
        lm = jnp.where(strict, beta[:, 0:C] * kk * decay, 0.0)
        tinv = _unit_lower_inverse(lm)
        eg = jnp.exp(gc)
        rhs = jnp.concatenate([k * beta * eg, v * beta], axis=-1)
        wu = _dot_hi(tinv, rhs)
        g_last = gc[last:last + 1, :]
        wq_ref[d, 0:C, :] = wu[:, :HEAD_DIM].astype(BF16)
        wq_ref[d, C:2 * C, :] = (q * eg).astype(BF16)
        u_ref[d] = wu[:, HEAD_DIM:]
        kd = k * jnp.exp(g_last - gc)
        qkkd_ref[d, 0, 0:C, :] = (qk * decay).astype(BF16)
        qkkd_ref[d, 0, C:C + HEAD_DIM, :] = jnp.transpose(kd).astype(BF16)
        gl_ref[d, 0] = jnp.broadcast_to(jnp.exp(g_last), (8, HEAD_DIM))


def _gdn_prep(qkv, colb, rowb):
    M = qkv.shape[0]
    nch = M // CHUNK
    H = GDN_HEADS
    C = CHUNK
    cb_spec = lambda ch0: pl.BlockSpec((1, C, LANES), lambda c, h: (ch0 + h, c, 0))
    rb_spec = lambda ch0: pl.BlockSpec((1, 1, 8, C), lambda c, h: (ch0 + h, c, 0, 0))
    return pl.pallas_call(
        _gdn_prep_kernel,
        grid=(nch, H),
        in_specs=[
            pl.BlockSpec((C, HEAD_DIM), lambda c, h: (c, h)),
            pl.BlockSpec((C, HEAD_DIM), lambda c, h: (c, H + h)),
            pl.BlockSpec((C, HEAD_DIM), lambda c, h: (c, 2 * H + h)),
            cb_spec(0), cb_spec(16), cb_spec(8), cb_spec(24),
            rb_spec(16), rb_spec(24),
        ],
        out_specs=[
            pl.BlockSpec((2, 2 * C, HEAD_DIM), lambda c, h: (0, c, h)),
            pl.BlockSpec((2, C, HEAD_DIM), lambda c, h: (0, c, h)),
            pl.BlockSpec((2, 1, C + HEAD_DIM, C), lambda c, h: (0, h, c, 0)),
            pl.BlockSpec((2, 1, 8, HEAD_DIM), lambda c, h: (0, h, c, 0)),
        ],
        out_shape=[
            jax.ShapeDtypeStruct((2, 2 * M, H * HEAD_DIM), BF16),
            jax.ShapeDtypeStruct((2, M, H * HEAD_DIM), F32),
            jax.ShapeDtypeStruct((2, H, nch * (C + HEAD_DIM), C), BF16),
            jax.ShapeDtypeStruct((2, H, nch * 8, HEAD_DIM), F32),
        ],
        compiler_params=_cparams(("arbitrary", "arbitrary")),
    )(qkv, qkv, qkv, colb, colb, colb, colb, rowb, rowb)


def _gdn_state_kernel(wqf_ref, uf_ref, qkf_ref, glf_ref, wqb_ref, ub_ref, qkb_ref, glb_ref,
                      of_ref, ob_ref, s_ref):
    C = CHUNK
    H = GDN_HEADS

    @pl.when(pl.program_id(1) == 0)
    def _():
        s_ref[...] = jnp.zeros(s_ref.shape, F32)

    for d, (wq_ref, u_ref, qk_ref, gl_ref, o_ref) in enumerate(
            ((wqf_ref, uf_ref, qkf_ref, glf_ref, of_ref), (wqb_ref, ub_ref, qkb_ref, glb_ref, ob_ref))):
        for h in range(H):
            cols = slice(h * HEAD_DIM, (h + 1) * HEAD_DIM)
            S = s_ref[d * H + h]
            m1 = _dot(wq_ref[0, :, cols], S.astype(BF16))
            v_new = u_ref[0, :, cols] - m1[0:C]
            m2 = _dot(qk_ref[0, h], v_new.astype(BF16))
            o_ref[:, cols] = m1[C:2 * C] + m2[0:C]
            s_ref[d * H + h] = S * gl_ref[0, h][0:1, :] + m2[C:]


def _seq_chunk_maps(B, S, L):
    C = CHUNK
    lc, sc = L // C, S // C
    ctx0 = (B * S) // C

    def fwd(b, c):
        return jnp.where(c < lc, ctx0 + b * lc + c, b * sc + (c - lc))

    def bwd(b, c):
        return jnp.where(c < lc, ctx0 + b * lc + (lc - 1 - c), b * sc + (sc - 1 - (c - lc)))

    return fwd, bwd, lc + sc


def _gdn_state(wq, u, qkkd, gl, *, B, S, L):
    C = CHUNK
    H = GDN_HEADS
    M = u.shape[1]
    fwd, bwd, nc = _seq_chunk_maps(B, S, L)

    def specs(d, pos):
        return [
            pl.BlockSpec((1, 2 * C, H * HEAD_DIM), lambda b, c: (d, pos(b, c), 0)),
            pl.BlockSpec((1, C, H * HEAD_DIM), lambda b, c: (d, pos(b, c), 0)),
            pl.BlockSpec((1, H, C + HEAD_DIM, C), lambda b, c: (d, 0, pos(b, c), 0)),
            pl.BlockSpec((1, H, 8, HEAD_DIM), lambda b, c: (d, 0, pos(b, c), 0)),
        ]

    return pl.pallas_call(
        _gdn_state_kernel,
        grid=(B, nc),
        in_specs=specs(0, fwd) + specs(1, bwd),
        out_specs=[pl.BlockSpec((C, H * HEAD_DIM), lambda b, c: (fwd(b, c), 0)),
                   pl.BlockSpec((C, H * HEAD_DIM), lambda b, c: (bwd(b, c), 0))],
        out_shape=[jax.ShapeDtypeStruct((M, H * HEAD_DIM), F32)] * 2,
        scratch_shapes=[pltpu.VMEM((2 * H, HEAD_DIM, HEAD_DIM), F32)],
        compiler_params=_cparams(("arbitrary", "arbitrary")),
    )(wq, u, qkkd, gl, wq, u, qkkd, gl)


def _gla_kernel(qf_ref, kf_ref, vf_ref, zf_ref, qb_ref, kb_ref, vb_ref, zb_ref, wgk_ref, bgk_ref,
                of_ref, ob_ref, s_ref):
    C = CHUNK
    r = lax.broadcasted_iota(jnp.int32, (C, C), 0)
    c = lax.broadcasted_iota(jnp.int32, (C, C), 1)

    @pl.when(pl.program_id(1) == 0)
    def _():
        s_ref[...] = jnp.zeros(s_ref.shape, F32)

    for d, (q_ref, k_ref, v_ref, z_ref, o_ref) in enumerate(
            ((qf_ref, kf_ref, vf_ref, zf_ref, of_ref), (qb_ref, kb_ref, vb_ref, zb_ref, ob_ref))):
        upper = d == 1
        incl = (c >= r) if upper else (c <= r)
        tri = jnp.where(incl, 1.0, 0.0).astype(BF16)
        pre = _dot_hi(z_ref[...], wgk_ref[d]) + bgk_ref[d]
        gk = (jnp.minimum(pre, 0.0) - jnp.log(1.0 + jnp.exp(-jnp.abs(pre)))) * (1.0 / GLA_GATE_NORM)
        bcum = _dot_exact_lhs(tri, gk)
        mid = C // 2 - 1 if upper else C // 2
        last = 0 if upper else C - 1
        for h in range(GLA_HEADS):
            kc = slice(h * GLA_DK, (h + 1) * GLA_DK)
            vc = slice(h * GLA_DV, (h + 1) * GLA_DV)
            q = q_ref[:, kc].astype(F32) * (GLA_DK ** -0.5)
            k = k_ref[:, kc].astype(F32)
            v = v_ref[:, vc]
            b = bcum[:, kc]
            b_mid = b[mid:mid + 1, :]
            b_last = b[last:last + 1, :]
            scores = _dot_nt((q * jnp.exp(b - b_mid)).astype(BF16), (k * jnp.exp(b_mid - b)).astype(BF16))
            scores = jnp.where(incl, scores, 0.0)
            S = s_ref[d * GLA_HEADS + h]
            o = _dot(scores.astype(BF16), v) + _dot((q * jnp.exp(b)).astype(BF16), S.astype(BF16))
            o_ref[:, vc] = o
            kd = (k * jnp.exp(b_last - b)).astype(BF16)
            dl = jnp.transpose(jnp.broadcast_to(jnp.exp(b_last), (GLA_DK, GLA_DK)))
            s_ref[d * GLA_HEADS + h] = S * jnp.concatenate([dl, dl], axis=-1) + _dot_tn(kd, v)


def _gla(z2, zg, wgk_pad, bgk, *, B, S, L):
    C = CHUNK
    M = z2.shape[0]
    fwd, bwd, nc = _seq_chunk_maps(B, S, L)
    kw = GLA_HEADS * GLA_DK
    vw = GLA_HEADS * GLA_DV

    def specs(pos):
        return [
            pl.BlockSpec((C, kw), lambda b, c: (pos(b, c), 0)),
            pl.BlockSpec((C, kw), lambda b, c: (pos(b, c), 1)),
            pl.BlockSpec((C, vw), lambda b, c: (pos(b, c), 1)),
            pl.BlockSpec((C, LANES), lambda b, c: (pos(b, c), 0)),
        ]

    return pl.pallas_call(
        _gla_kernel,
        grid=(B, nc),
        in_specs=specs(fwd) + specs(bwd) + [
            pl.BlockSpec((2, LANES, kw), lambda b, c: (0, 0, 0)),
            pl.BlockSpec((2, 1, kw), lambda b, c: (0, 0, 0)),
        ],
        out_specs=[pl.BlockSpec((C, vw), lambda b, c: (fwd(b, c), 0)),
                   pl.BlockSpec((C, vw), lambda b, c: (bwd(b, c), 0))],
        out_shape=[jax.ShapeDtypeStruct((M, vw), F32)] * 2,
        scratch_shapes=[pltpu.VMEM((2 * GLA_HEADS, GLA_DK, GLA_DV), F32)],
        compiler_params=_cparams(("arbitrary", "arbitrary")),
    )(z2, z2, z2, zg, z2, z2, z2, zg, wgk_pad, bgk)


def _headnorm_kernel(of_ref, ob_ref, z_ref, g_ref, o_ref, *, hd):
    n = of_ref.shape[1] // hd
    for h in range(n):
        cols = slice(h * hd, (h + 1) * hd)
        o = of_ref[:, cols] + ob_ref[:, cols]
        y = o * lax.rsqrt(jnp.mean(o * o, axis=-1, keepdims=True) + RMS_EPS) * g_ref[...]
        o_ref[:, cols] = (y * _silu(z_ref[:, cols].astype(F32))).astype(o_ref.dtype)


def _headnorm(of, ob, z, zcol, gain, *, hd, tm=256):
    M, W = of.shape
    return pl.pallas_call(
        functools.partial(_headnorm_kernel, hd=hd),
        grid=(M // tm,),
        in_specs=[pl.BlockSpec((tm, W), lambda i: (i, 0)),
                  pl.BlockSpec((tm, W), lambda i: (i, 0)),
                  pl.BlockSpec((tm, W), lambda i: (i, zcol)),
                  pl.BlockSpec((1, hd), lambda i: (0, 0))],
        out_specs=pl.BlockSpec((tm, W), lambda i: (i, 0)),
        out_shape=jax.ShapeDtypeStruct((M, W), BF16),
        compiler_params=_cparams(("arbitrary",)),
    )(of, ob, z, gain.reshape(1, hd))


def _final_kernel(rowid_ref, *refs, with_y, norm):
    del rowid_ref
    refs = list(refs)
    x = refs.pop(0)[...]
    if with_y:
        y0_ref, y1_ref, gt_ref = refs[:3]
        refs = refs[3:]
        x = x + gt_ref[0] * (y0_ref[...] + y1_ref[...])
    if norm:
        g_ref = refs.pop(0)
        x = x * lax.rsqrt(jnp.mean(x * x, axis=-1, keepdims=True) + RMS_EPS) * g_ref[...]
    refs[0][...] = x


def _final(X, ys, mod3, rowid, layer, k_gate, g_final, *, tm, grid_m):
    D = X.shape[1]
    row_spec = pl.BlockSpec((tm, D), lambda i, r: (i, 0))
    args, in_specs = [X], [row_spec]
    if ys is not None:
        args += [ys[0], ys[1], mod3]
        in_specs += [row_spec, row_spec,
                     pl.BlockSpec((1, 1, D), lambda i, r: (layer * 16 + r[i], 0, k_gate))]
    if g_final is not None:
        args.append(g_final.reshape(1, D))
        in_specs.append(pl.BlockSpec((1, D), lambda i, r: (0, 0)))
    return pl.pallas_call(
        functools.partial(_final_kernel, with_y=ys is not None, norm=g_final is not None),
        grid_spec=pltpu.PrefetchScalarGridSpec(
            num_scalar_prefetch=1, grid=(grid_m,), in_specs=in_specs, out_specs=row_spec),
        out_shape=jax.ShapeDtypeStruct((grid_m * tm, D), F32),
        compiler_params=_cparams(("arbitrary",)),
    )(rowid, *args)


def _rope_tables(S):
    rows = S // GRID_W
    row = jnp.repeat(jnp.arange(rows, dtype=F32), GRID_W)
    col = jnp.tile(jnp.arange(GRID_W, dtype=F32), rows)
    inv_freq = ROPE_THETA ** (-jnp.arange(ROPE_FREQS, dtype=F32) / ROPE_FREQS)
    ar = row[:, None] * inv_freq
    ac = col[:, None] * inv_freq
    cos = jnp.concatenate([jnp.cos(ar), jnp.cos(ar), jnp.cos(ac), jnp.cos(ac)], axis=-1)
    sin = jnp.concatenate([-jnp.sin(ar), jnp.sin(ar), -jnp.sin(ac), jnp.sin(ac)], axis=-1)
    return cos, sin


def _stream_rowid(B, S, L, tm, grid_m):
    i = jnp.arange(grid_m, dtype=jnp.int32) * tm
    return jnp.where(i < B * S, i // S, B).astype(jnp.int32)


def _moe_plan(logits, T, tm, n_tiles):
    top_val, top_idx = lax.top_k(logits, TOP_K)
    top_w = jax.nn.softmax(top_val, axis=-1)
    e_flat = top_idx.reshape(-1).astype(jnp.int32)
    w_flat = top_w.reshape(-1)
    order = jnp.argsort(e_flat, stable=True)
    e_sorted = e_flat[order]
    sizes = jnp.bincount(e_flat, length=N_EXPERTS).astype(jnp.int32)
    start = jnp.cumsum(sizes) - sizes
    padded = ((sizes + tm - 1) // tm) * tm
    pend = jnp.cumsum(padded)
    pstart = pend - padded
    rank = jnp.arange(T * TOP_K, dtype=jnp.int32) - start[e_sorted]
    dest_sorted = pstart[e_sorted] + rank
    P = n_tiles * tm
    row_token = jnp.zeros((P,), jnp.int32).at[dest_sorted].set((order // TOP_K).astype(jnp.int32))
    row_w = jnp.zeros((P,), F32).at[dest_sorted].set(w_flat[order])
    dest = jnp.zeros((T * TOP_K,), jnp.int32).at[order].set(dest_sorted)
    tile0 = jnp.arange(n_tiles, dtype=jnp.int32) * tm
    tile_e = jnp.minimum(jnp.searchsorted(pend, tile0, side='right'), N_EXPERTS - 1).astype(jnp.int32)
    valid = (tile0 < pend[-1]).astype(jnp.int32)
    last_e = tile_e[jnp.maximum(pend[-1] // tm - 1, 0)]
    tile_e = jnp.where(valid == 1, tile_e, last_e)
    first = jnp.concatenate([jnp.ones((1,), jnp.int32), (tile_e[1:] != tile_e[:-1]).astype(jnp.int32)])
    return row_token, row_w, dest.reshape(T, TOP_K), tile_e, first, valid


def kernel(x, c, ctx, c_ctx, w_ada, b_ada, g_norm_mix, g_norm_ffn, w_in, a_q_gain, a_k_gain, b_sink,
           gdn_conv, gdn_a_log, gdn_dt_bias, gdn_norm_gain, gla_w_gk, gla_b_gk, gla_norm_gain,
           w_branch, w_merge, w_out, w1_dense, w3_dense, w2_dense, w_router, w1_moe, w3_moe, w2_moe,
           g_final):
    B, S, D = x.shape
    L = ctx.shape[1]
    depth = w_in.shape[0]
    n_lat = B * S
    M = n_lat + B * L
    TM = 512
    assert D == D_MODEL and S % CONV_ROWS == 0 and L % CONV_ROWS == 0 and n_lat % TM == 0
    assert (B * L) % TM == 0 and S % TM == 0 and S >= 3 * Q_BLOCK and n_lat % L == 0 and B <= 15

    cond = jnp.zeros((16, D), F32).at[:B].set(c).at[B].set(c_ctx)
    cond2 = jnp.concatenate([cond] * depth, axis=0)
    (mod,) = _panel_call(
        _ep_ada, grid_n=(6 * D) // 512, grid_m=depth,
        lhs=[(cond2, (16, D), lambda j, i, e, f, v, r: (i, 0))],
        weights=[(w_ada, (1, D, 512), lambda j, i, e, f, v, r: (e[i], 0, j))],
        extras=[(b_ada.reshape(depth, 1, 6 * D), (1, 1, 512), lambda j, i, e, f, v, r: (e[i], 0, j))],
        outs=[(jax.ShapeDtypeStruct((16 * depth, 6 * D), F32), (16, 512), lambda j, i, e, f, v, r: (i, j))],
        eid=jnp.arange(depth, dtype=jnp.int32), first=jnp.ones((depth,), jnp.int32))
    mod3 = mod.reshape(16 * depth, 1, 6 * D)

    cos, sin = _rope_tables(S)
    X = jnp.concatenate([x.reshape(n_lat, D), ctx.reshape(B * L, D)], axis=0)
    gm_all = M // TM
    gm_lat = n_lat // TM
    rowid = _stream_rowid(B, S, L, TM, gm_all)
    rowid256 = _stream_rowid(B, S, L, 256, M // 256)

    blk = jnp.arange(M // CONV_ROWS, dtype=jnp.int32) * CONV_ROWS
    seq_len = jnp.where(blk < n_lat, S, L)
    seq_off = jnp.where(blk < n_lat, blk, blk - n_lat)
    seq_first = (seq_off % seq_len == 0).astype(jnp.int32)
    seq_last = ((seq_off + CONV_ROWS) % seq_len == 0).astype(jnp.int32)

    out = None
    for layer in range(depth):
        last = layer == depth - 1
        gm = gm_lat if last else gm_all
        h = _modulate(X, g_norm_mix, mod3, rowid256, layer, 0, 1, tm=256, grid_m=M // 256)
        z1 = _matmul_stream(h, w_in, layer, 0, Z1_WIDTH, tm=TM, tn=1024, out_dtype=BF16)
        w_d = w_in[layer][:, COL_D_Q:COL_D_GATE]
        z2 = _matmul_stream(h, w_d, layer, 0, Z2_WIDTH, tm=TM, tn=1024, out_dtype=BF16)
        w_g = jnp.concatenate([w_in[layer][:, COL_C_BETA:COL_D_Q], w_in[layer][:, COL_D_GATE:],
                               jnp.zeros((D, LANES - 64), F32)], axis=1)
        zg = _matmul_stream(h, w_g, layer, 0, LANES, tm=TM, tn=LANES, out_dtype=F32)

        oa = _attention(z1, cos, sin, a_q_gain[layer], a_k_gain[layer], None, mode='all',
                        B=B, S=S, L=L, qcol=0, kcol=8, vcol=10)
        ob = _attention(z1, cos, sin, None, None, b_sink[layer], mode='band',
                        B=B, S=S, L=L, qcol=3, kcol=20, vcol=22)
        if not last:
            oa_c = _attention(z1, cos, sin, a_q_gain[layer], a_k_gain[layer], None, mode='ctx',
                              B=B, S=S, L=L, qcol=0, kcol=8, vcol=10)
            ob_c = _attention(z1, cos, sin, None, None, b_sink[layer], mode='ctx',
                              B=B, S=S, L=L, qcol=3, kcol=20, vcol=22)
            oa = jnp.concatenate([oa, oa_c], axis=0)
            ob = jnp.concatenate([ob, ob_c], axis=0)

        qkv = _gdn_conv(z1, gdn_conv[layer], seq_first, seq_last)
        a_row = jnp.zeros((1, LANES), F32).at[0, 16:32].set(-jnp.exp(gdn_a_log[layer]).reshape(-1))
        dtb_row = jnp.zeros((1, LANES), F32).at[0, 16:32].set(gdn_dt_bias[layer].reshape(-1))
        gates = _gdn_gates(zg, a_row, dtb_row)
        gt = gates[:, :32].T
        colb = jnp.broadcast_to(gt[:, :, None], (32, M, LANES))
        rowb = jnp.broadcast_to(gt.reshape(32, M // CHUNK, 1, CHUNK), (32, M // CHUNK, 8, CHUNK))
        wq, u, qkkd, gl = _gdn_prep(qkv, colb, rowb)
        oc_f, oc_b = _gdn_state(wq, u, qkkd, gl, B=B, S=S, L=L)
        oc = _headnorm(oc_f, oc_b, z1, 6144 // BRANCH_WIDTH, gdn_norm_gain[layer], hd=HEAD_DIM)

        wgk_pad = jnp.zeros((2, LANES, GLA_HEADS * GLA_DK), F32)
        wgk_pad = wgk_pad.at[0, 32:48].set(gla_w_gk[layer, 0]).at[1, 48:64].set(gla_w_gk[layer, 1])
        od_f, od_b = _gla(z2, zg, wgk_pad, gla_b_gk[layer].reshape(2, 1, -1), B=B, S=S, L=L)
        od = _headnorm(od_f, od_b, z2, 2048 // BRANCH_WIDTH, gla_norm_gain[layer], hd=GLA_DV)

        tn_m = 256
        branches = [oa, ob, oc, od]
        lhs = [(h, (TM, D), lambda j, i, e, f, v, r: (i, 0))]
        lhs += [(o, (TM, BRANCH_WIDTH), lambda j, i, e, f, v, r: (i, 0)) for o in branches]
        wts = [(w_merge, (1, D, tn_m), lambda j, i, e, f, v, r, n=n: (layer, 0, n * (D // tn_m) + j))
               for n in range(N_BRANCH)]
        wts += [(w_branch, (1, 1, BRANCH_WIDTH, tn_m), lambda j, i, e, f, v, r, n=n: (layer, n, 0, j))
                for n in range(N_BRANCH)]
        (merged,) = _panel_call(
            _ep_merge, grid_n=D // tn_m, grid_m=gm, lhs=lhs, weights=wts, extras=[],
            outs=[(jax.ShapeDtypeStruct((gm * TM, D), BF16), (TM, tn_m), lambda j, i, e, f, v, r: (i, j))])

        def resid(xin, w, wmap, kdim, k_gate, tn):
            (res,) = _panel_call(
                _ep_resid, grid_n=D // tn, grid_m=gm,
                lhs=[(xin, (TM, kdim), lambda j, i, e, f, v, r: (i, 0))],
                weights=[(w, (1, kdim, tn), wmap)],
                extras=[(X, (TM, tn), lambda j, i, e, f, v, r: (i, j)),
                        (mod3, (1, 1, tn), lambda j, i, e, f, v, r: (layer * 16 + r[i], 0, k_gate * (D // tn) + j))],
                outs=[(jax.ShapeDtypeStruct((gm * TM, D), F32), (TM, tn), lambda j, i, e, f, v, r: (i, j))],
                rowid=rowid[:gm])
            return res

        X = resid(merged, w_out, lambda j, i, e, f, v, r: (layer, 0, j), D, 2, 512)

        if layer % 2 == 0:
            li = layer // 2
            h2 = _modulate(X, g_norm_ffn, mod3, rowid256, layer, 3, 4, tm=256, grid_m=(gm * TM) // 256)
            d_ff = w1_dense.shape[2]
            (uu,) = _panel_call(
                _ep_swiglu, grid_n=d_ff // 512, grid_m=gm,
                lhs=[(h2, (TM, D), lambda j, i, e, f, v, r: (i, 0))],
                weights=[(w1_dense, (1, D, 512), lambda j, i, e, f, v, r: (li, 0, j)),
                         (w3_dense, (1, D, 512), lambda j, i, e, f, v, r: (li, 0, j))],
                extras=[],
                outs=[(jax.ShapeDtypeStruct((gm * TM, d_ff), BF16), (TM, 512), lambda j, i, e, f, v, r: (i, j))])
            X = resid(uu, w2_dense, lambda j, i, e, f, v, r: (li, 0, j), d_ff, 5, 256)
            if last:
                out = _final(X, None, mod3, rowid256, layer, 5, g_final, tm=256, grid_m=n_lat // 256)
        else:
            li = layer // 2
            T = gm * TM
            wr_pad = jnp.zeros((D, LANES), F32).at[:, :N_EXPERTS].set(w_router[li])
            h2, logits = _modulate(X, g_norm_ffn, mod3, rowid256, layer, 3, 4, tm=256,
                                   grid_m=T // 256, w_router=wr_pad)
            tm_e = 512
            n_tiles = (T * TOP_K) // tm_e + N_EXPERTS
            row_token, row_w, dest, tile_e, first, valid = _moe_plan(logits[:, :N_EXPERTS], T, tm_e, n_tiles)
            xg = jnp.take(h2, row_token, axis=0)
            d_ffe = w1_moe.shape[3]
            (uu,) = _panel_call(
                _ep_swiglu, grid_n=d_ffe // 512, grid_m=n_tiles,
                lhs=[(xg, (tm_e, D), lambda j, i, e, f, v, r: (i, 0))],
                weights=[(w1_moe, (1, 1, D, 512), lambda j, i, e, f, v, r: (li, e[i], 0, j)),
                         (w3_moe, (1, 1, D, 512), lambda j, i, e, f, v, r: (li, e[i], 0, j))],
                extras=[],
                outs=[(jax.ShapeDtypeStruct((n_tiles * tm_e, d_ffe), BF16), (tm_e, 512),
                       lambda j, i, e, f, v, r: (i, j))],
                eid=tile_e, first=first, valid=valid)
            tm_d = 256
            tn_d = 512
            rep = tm_e // tm_d
            first_d = jnp.repeat(first, rep) * (jnp.arange(n_tiles * rep, dtype=jnp.int32) % rep == 0)
            (yo,) = _panel_call(
                _ep_rowscale, grid_n=D // tn_d, grid_m=n_tiles * rep,
                lhs=[(uu, (tm_d, d_ffe), lambda j, i, e, f, v, r: (i, 0))],
                weights=[(w2_moe, (1, 1, d_ffe, tn_d), lambda j, i, e, f, v, r: (li, e[i], 0, j))],
                extras=[(row_w.reshape(-1, 1), (tm_d, 1), lambda j, i, e, f, v, r: (i, 0))],
                outs=[(jax.ShapeDtypeStruct((n_tiles * tm_e, D), F32), (tm_d, tn_d),
                       lambda j, i, e, f, v, r: (i, j))],
                eid=jnp.repeat(tile_e, rep), first=first_d.astype(jnp.int32), valid=jnp.repeat(valid, rep))
            ys = (jnp.take(yo, dest[:, 0], axis=0), jnp.take(yo, dest[:, 1], axis=0))
            res = _final(X, ys, mod3, rowid256, layer, 5, g_final if last else None, tm=256,
                         grid_m=T // 256)
            if last:
                out = res
            else:
                X = res
    return out.reshape(B, S, D)
```

```python
import functools

import jax
import jax.numpy as jnp
from jax import lax
from jax.experimental import pallas as pl
from jax.experimental.pallas import tpu as pltpu

F32 = jnp.float32
BF16 = jnp.bfloat16

D_MODEL = 2048
HEAD_DIM = 128
GRID_W = 64
ROPE_THETA = 10000.0
ROPE_FREQS = HEAD_DIM // 4
RMS_EPS = 1e-6
BRANCH_WIDTH = D_MODEL // 2
KV_GROUP = 4
ATTN_KV_HEADS = 2
WINDOW = 128
Q_BLOCK = 128
GDN_HEADS = 8
GLA_HEADS = 4
GLA_DK = 128
GLA_DV = 256
GLA_RANK = 16
GLA_GATE_NORM = 16.0
CHUNK = 64
N_EXPERTS = 8
TOP_K = 2
N_BRANCH = 4
NEG_BIG = -1e30
LOG2E = 1.4426950408889634

COL_C_BETA = 7168
COL_D_Q = 7200
COL_D_GATE = 10272
PROJ_WIDTH = 10304
Z1_WIDTH = 7168
Z2_WIDTH = 3072

VMEM_LIMIT = 56 * 1024 * 1024
LANES = 128


def _cparams(sem):
    return pltpu.CompilerParams(dimension_semantics=sem, vmem_limit_bytes=VMEM_LIMIT)


def _silu(x):
    return x * (1.0 / (1.0 + jnp.exp(-x)))


def _sigmoid(x):
    return 1.0 / (1.0 + jnp.exp(-x))


def _split3(x):
    x1 = x.astype(BF16)
    r1 = x - x1.astype(F32)
    x2 = r1.astype(BF16)
    x3 = (r1 - x2.astype(F32)).astype(BF16)
    return x1, x2, x3


def _dot(a, b):
    return jnp.dot(a, b, preferred_element_type=F32)


def _dot_nt(a, b):
    return lax.dot_general(a, b, (((1,), (1,)), ((), ())), preferred_element_type=F32)


def _dot_tn(a, b):
    return lax.dot_general(a, b, (((0,), (0,)), ((), ())), preferred_element_type=F32)


def _dot_exact_lhs(tri_bf16, x):
    x1, x2, x3 = _split3(x)
    return _dot(tri_bf16, x1) + _dot(tri_bf16, x2) + _dot(tri_bf16, x3)


def _dot_hi(a, b):
    a1 = a.astype(BF16)
    a2 = (a - a1.astype(F32)).astype(BF16)
    b1 = b.astype(BF16)
    b2 = (b - b1.astype(F32)).astype(BF16)
    return _dot(a1, b1) + _dot(a1, b2) + _dot(a2, b1)


_CAST_ROWS = 256


def _panel_kernel(eid_ref, first_ref, valid_ref, rowid_ref, *refs, n_lhs, n_w, n_ex, n_out,
                  w_rows, epilogue):
    del eid_ref, rowid_ref
    lhs = refs[:n_lhs]
    ws = refs[n_lhs:n_lhs + n_w]
    exs = refs[n_lhs + n_w:n_lhs + n_w + n_ex]
    outs = refs[n_lhs + n_w + n_ex:n_lhs + n_w + n_ex + n_out]
    wbs = refs[n_lhs + n_w + n_ex + n_out:]
    i = pl.program_id(1)

    @pl.when(first_ref[i] == 1)
    def _():
        for w_ref, wb_ref, rows in zip(ws, wbs, w_rows):
            lead = (0,) * (len(w_ref.shape) - 2)

            def cast(r, carry, w_ref=w_ref, wb_ref=wb_ref, lead=lead):
                rr = pl.multiple_of(r * _CAST_ROWS, _CAST_ROWS)
                wb_ref[pl.ds(rr, _CAST_ROWS), :] = w_ref[lead + (pl.ds(rr, _CAST_ROWS), slice(None))].astype(BF16)
                return carry

            lax.fori_loop(0, rows // _CAST_ROWS, cast, 0)

    @pl.when(valid_ref[i] == 1)
    def _():
        epilogue(lhs, wbs, exs, outs)

    @pl.when(valid_ref[i] == 0)
    def _():
        for o in outs:
            o[...] = jnp.zeros(o.shape, o.dtype)


def _panel_call(epilogue, *, name, grid_n, grid_m, lhs, weights, extras, outs, eid=None, first=None,
                valid=None, rowid=None):
    if eid is None:
        eid = jnp.zeros((grid_m,), jnp.int32)
    if first is None:
        first = jnp.zeros((grid_m,), jnp.int32).at[0].set(1)
    if valid is None:
        valid = jnp.ones((grid_m,), jnp.int32)
    if rowid is None:
        rowid = jnp.zeros((grid_m,), jnp.int32)
    in_arrays, in_specs = [], []
    for arr, blk, imap in list(lhs) + list(weights) + list(extras):
        in_arrays.append(arr)
        in_specs.append(pl.BlockSpec(blk, imap))
    out_shapes = [o[0] for o in outs]
    out_specs = [pl.BlockSpec(o[1], o[2]) for o in outs]
    w_rows = [blk[-2] for _, blk, _ in weights]
    scratch = [pltpu.VMEM((blk[-2], blk[-1]), BF16) for _, blk, _ in weights]
    kern = functools.partial(_panel_kernel, n_lhs=len(lhs), n_w=len(weights), n_ex=len(extras),
                             n_out=len(outs), w_rows=w_rows, epilogue=epilogue)
    res = pl.pallas_call(
        kern,
        grid_spec=pltpu.PrefetchScalarGridSpec(
            num_scalar_prefetch=4, grid=(grid_n, grid_m),
            in_specs=in_specs, out_specs=out_specs, scratch_shapes=scratch),
        out_shape=out_shapes,
        compiler_params=_cparams(("arbitrary", "arbitrary")),
        name=name,
    )(eid, first, valid, rowid, *in_arrays)
    return res


def _ep_plain(lhs, wbs, exs, outs):
    outs[0][...] = _dot(lhs[0][...], wbs[0][...]).astype(outs[0].dtype)


def _ep_ada(lhs, wbs, exs, outs):
    x = _silu(lhs[0][...]).astype(BF16)
    outs[0][...] = _dot(x, wbs[0][...]) + exs[0][0]


def _ep_swiglu(lhs, wbs, exs, outs):
    x = lhs[0][...]
    a1 = _dot(x, wbs[0][...])
    a3 = _dot(x, wbs[1][...])
    outs[0][...] = (_silu(a1) * a3).astype(outs[0].dtype)


def _ep_resid(lhs, wbs, exs, outs):
    outs[0][...] = exs[0][...] + exs[1][0] * _dot(lhs[0][...], wbs[0][...])


def _ep_rowscale(lhs, wbs, exs, outs):
    outs[0][...] = exs[0][...] * _dot(lhs[0][...], wbs[0][...])


def _ep_merge(lhs, wbs, exs, outs):
    h = lhs[0][...]
    acc = None
    for n in range(N_BRANCH):
        gate = _sigmoid(_dot(h, wbs[n][...]))
        y = _dot(lhs[1 + n][...], wbs[N_BRANCH + n][...])
        acc = gate * y if acc is None else acc + gate * y
    outs[0][...] = acc.astype(outs[0].dtype)


def _matmul_stream(x, w, layer, col0, n_cols, *, name, tm, tn, out_dtype, grid_m=None):
    M, K = x.shape
    grid_m = M // tm if grid_m is None else grid_m
    jb = col0 // tn
    if w.ndim == 3:
        wspec = (w, (1, K, tn), lambda j, i, e, f, v, r: (layer, 0, jb + j))
    else:
        wspec = (w, (K, tn), lambda j, i, e, f, v, r: (0, jb + j))
    (out,) = _panel_call(
        _ep_plain, name=name, grid_n=n_cols // tn, grid_m=grid_m,
        lhs=[(x, (tm, K), lambda j, i, e, f, v, r: (i, 0))],
        weights=[wspec], extras=[],
        outs=[(jax.ShapeDtypeStruct((grid_m * tm, n_cols), out_dtype), (tm, tn),
               lambda j, i, e, f, v, r: (i, j))])
    return out


def _modulate_kernel(rowid_ref, x_ref, g_ref, sh_ref, sc_ref, *rest, with_router):
    del rowid_ref
    x = x_ref[...]
    y = x * lax.rsqrt(jnp.mean(x * x, axis=-1, keepdims=True) + RMS_EPS)
    h = (y * g_ref[0]) * (1.0 + sc_ref[0]) + sh_ref[0]
    if with_router:
        wr_ref, h_ref, lg_ref = rest
        h_ref[...] = h.astype(BF16)
        lg_ref[...] = _dot_hi(h, wr_ref[...])
    else:
        (h_ref,) = rest
        h_ref[...] = h.astype(BF16)


def _modulate(X, gain, mod3, rowid, layer, k_shift, k_scale, *, tm, grid_m, w_router=None):
    D = X.shape[1]
    with_router = w_router is not None
    in_specs = [
        pl.BlockSpec((tm, D), lambda i, r: (i, 0)),
        pl.BlockSpec((1, 1, D), lambda i, r: (layer, 0, 0)),
        pl.BlockSpec((1, 1, D), lambda i, r: (layer * 16 + r[i], 0, k_shift)),
        pl.BlockSpec((1, 1, D), lambda i, r: (layer * 16 + r[i], 0, k_scale)),
    ]
    args = [X, gain.reshape(gain.shape[0], 1, D), mod3, mod3]
    out_shape = [jax.ShapeDtypeStruct((grid_m * tm, D), BF16)]
    out_specs = [pl.BlockSpec((tm, D), lambda i, r: (i, 0))]
    if with_router:
        in_specs.append(pl.BlockSpec((D, LANES), lambda i, r: (0, 0)))
        args.append(w_router)
        out_shape.append(jax.ShapeDtypeStruct((grid_m * tm, LANES), F32))
        out_specs.append(pl.BlockSpec((tm, LANES), lambda i, r: (i, 0)))
    res = pl.pallas_call(
        functools.partial(_modulate_kernel, with_router=with_router),
        grid_spec=pltpu.PrefetchScalarGridSpec(
            num_scalar_prefetch=1, grid=(grid_m,), in_specs=in_specs, out_specs=out_specs),
        out_shape=out_shape,
        compiler_params=_cparams(("arbitrary",)),
        name="modulate_router" if with_router else "modulate",
    )(rowid, *args)
    return res if with_router else res[0]


def _swap_halves(x):
    lane = lax.broadcasted_iota(jnp.int32, x.shape, x.ndim - 1)
    return jnp.where((lane % 64) < 32, pltpu.roll(x, 96, x.ndim - 1), pltpu.roll(x, 32, x.ndim - 1))


def _rope(x, cos, sin_signed):
    return x * cos + _swap_halves(x) * sin_signed


def _head_rms(x, gain_row):
    return x * lax.rsqrt(jnp.mean(x * x, axis=-1, keepdims=True) + RMS_EPS) * gain_row


ATTN_ALL_ROWS = 256


def _lane_tile_max(s):
    m = s[:, 0:LANES]
    for k in range(1, s.shape[1] // LANES):
        m = jnp.maximum(m, s[:, k * LANES:(k + 1) * LANES])
    return m


def _attn_kernel(*refs, mode, norm, sink, tq, S, L):
    refs = list(refs)
    sink_ref = refs.pop(0) if sink else None
    q_ref, kc_ref, vc_ref = refs[:3]
    refs = refs[3:]
    if mode != 'ctx':
        kl_ref, vl_ref, cos_ref, sin_ref = refs[:4]
        refs = refs[4:]
    if norm:
        qg_ref, kg_ref = refs[:2]
        refs = refs[2:]
    o_ref, qs_ref = refs[:2]
    kp_ref = refs[2] if len(refs) > 2 else None
    ve_ref = refs[3] if len(refs) > 3 else None
    g = pl.program_id(1)
    t = pl.program_id(2)
    rows = 256

    if kp_ref is not None:
        @pl.when(t == 0)
        def _():
            if mode != 'band':
                for r0 in range(0, L, rows):
                    kc = kc_ref[r0:r0 + rows, :].astype(F32)
                    if norm:
                        kc = _head_rms(kc, kg_ref[...])
                    kp_ref[r0:r0 + rows, :] = kc.astype(BF16)
            if mode != 'ctx':
                base = L if mode == 'all' else 0
                for r0 in range(0, S, rows):
                    kl = kl_ref[r0:r0 + rows, :].astype(F32)
                    if norm:
                        kl = _head_rms(kl, kg_ref[...])
                    kl = _rope(kl, cos_ref[r0:r0 + rows, :], sin_ref[r0:r0 + rows, :])
                    kp_ref[base + r0:base + r0 + rows, :] = kl.astype(BF16)
                ve_ref[0:L, 0:HEAD_DIM] = vc_ref[...]
                ve_ref[L:L + S, 0:HEAD_DIM] = vl_ref[...]
                ve_ref[:, HEAD_DIM:2 * HEAD_DIM] = jnp.ones((L + S, HEAD_DIM), BF16)

    scale = (HEAD_DIM ** -0.5) * LOG2E
    if mode != 'ctx':
        t0 = pl.multiple_of(t * tq, tq)
        cos_q = cos_ref[pl.ds(t0, tq), :]
        sin_q = sin_ref[pl.ds(t0, tq), :]
    if mode == 'band':
        nb = 3 * Q_BLOCK
        start = pl.multiple_of(jnp.clip(t * tq - WINDOW, 0, S - nb), Q_BLOCK)
        qpos = t * tq + lax.broadcasted_iota(jnp.int32, (tq, nb), 0)
        kpos = start + lax.broadcasted_iota(jnp.int32, (tq, nb), 1)
        in_band = jnp.abs(qpos - kpos) <= WINDOW

    def scores(j):
        qj = q_ref[:, j * HEAD_DIM:(j + 1) * HEAD_DIM].astype(F32)
        if norm:
            qj = _head_rms(qj, qg_ref[...])
        if mode != 'ctx':
            qj = _rope(qj, cos_q, sin_q)
        qs_ref[j * tq:(j + 1) * tq, :] = (qj * scale).astype(BF16)
        qj = qs_ref[j * tq:(j + 1) * tq, :]
        if mode == 'all':
            half = (L + S) // 2
            return _dot_nt(qj, kp_ref[0:half, :]), _dot_nt(qj, kp_ref[half:L + S, :])
        if mode == 'band':
            s_loc = _dot_nt(qj, kp_ref[pl.ds(start, nb), :])
            return _dot_nt(qj, kc_ref[...]), jnp.where(in_band, s_loc, NEG_BIG)
        return (_dot_nt(qj, kp_ref[...] if norm else kc_ref[...]),)

    def finish(j, sc):
        sink_j = sink_ref[g * KV_GROUP + j] * LOG2E if sink else None
        if mode == 'ctx':
            (s,) = sc
            m = jnp.max(s, axis=-1, keepdims=True)
            if sink:
                m = jnp.maximum(m, sink_j)
            p = jnp.exp2(s - m)
            l = jnp.sum(p, axis=-1, keepdims=True)
            if sink:
                l = l + jnp.exp2(sink_j - m)
            o = _dot(p.astype(BF16), vc_ref[...]) / l
        else:
            s_a, s_b = sc
            m = jnp.max(jnp.maximum(_lane_tile_max(s_a), _lane_tile_max(s_b)), axis=-1, keepdims=True)
            if sink:
                m = jnp.maximum(m, sink_j)
            p_a = jnp.exp2((s_a - m).astype(BF16))
            p_b = jnp.exp2((s_b - m).astype(BF16))
            if mode == 'all':
                half = (L + S) // 2
                oe = _dot(p_a, ve_ref[0:half, :]) + _dot(p_b, ve_ref[half:L + S, :])
            else:
                oe = _dot(p_a, ve_ref[0:L, :]) + _dot(p_b, ve_ref[pl.ds(L + start, nb), :])
            l = oe[:, HEAD_DIM:]
            if sink:
                l = l + jnp.exp2(sink_j - m)
            o = oe[:, :HEAD_DIM] / l
        o_ref[:, j * HEAD_DIM:(j + 1) * HEAD_DIM] = o.astype(o_ref.dtype)

    pending = scores(0)
    for j in range(1, KV_GROUP):
        nxt = scores(j)
        finish(j - 1, pending)
        pending = nxt
    finish(KV_GROUP - 1, pending)


def _attention(z1, cos, sin, q_gain, k_gain, sink_vec, *, mode, B, S, L, qcol, kcol, vcol):
    norm = q_gain is not None
    sink = sink_vec is not None
    lat0 = 0
    ctx0 = (B * S) // L
    if mode == 'ctx':
        tq, nt = L, 1
        q_map = lambda b, g, t: (ctx0 + b, qcol + g)
    else:
        tq = ATTN_ALL_ROWS if mode == 'all' else Q_BLOCK
        nt = S // tq
        q_map = lambda b, g, t: (lat0 + b * nt + t, qcol + g)
    qw = KV_GROUP * HEAD_DIM
    args, in_specs = [], []
    if sink:
        args.append(sink_vec)
        in_specs.append(pl.BlockSpec(memory_space=pltpu.SMEM))
    args += [z1, z1, z1]
    in_specs += [
        pl.BlockSpec((tq, qw), q_map),
        pl.BlockSpec((L, HEAD_DIM), lambda b, g, t: (ctx0 + b, kcol + g)),
        pl.BlockSpec((L, HEAD_DIM), lambda b, g, t: (ctx0 + b, vcol + g)),
    ]
    if mode != 'ctx':
        args += [z1, z1, cos, sin]
        in_specs += [
            pl.BlockSpec((S, HEAD_DIM), lambda b, g, t: (b, kcol + g)),
            pl.BlockSpec((S, HEAD_DIM), lambda b, g, t: (b, vcol + g)),
            pl.BlockSpec((S, HEAD_DIM), lambda b, g, t: (0, 0)),
            pl.BlockSpec((S, HEAD_DIM), lambda b, g, t: (0, 0)),
        ]
    if norm:
        args += [q_gain.reshape(1, HEAD_DIM), k_gain.reshape(1, HEAD_DIM)]
        in_specs += [pl.BlockSpec((1, HEAD_DIM), lambda b, g, t: (0, 0))] * 2
    scratch = [pltpu.VMEM((KV_GROUP * tq, HEAD_DIM), BF16)]
    assert not (mode == 'band' and norm)
    if mode == 'all':
        scratch.append(pltpu.VMEM((L + S, HEAD_DIM), BF16))
        scratch.append(pltpu.VMEM((L + S, 2 * HEAD_DIM), BF16))
    elif mode == 'band':
        scratch.append(pltpu.VMEM((S, HEAD_DIM), BF16))
        scratch.append(pltpu.VMEM((L + S, 2 * HEAD_DIM), BF16))
    elif norm:
        scratch.append(pltpu.VMEM((L, HEAD_DIM), BF16))
    n_rows = B * (L if mode == 'ctx' else S)
    if mode == 'ctx':
        o_map = lambda b, g, t: (b, g)
    else:
        o_map = lambda b, g, t: (b * nt + t, g)
    return pl.pallas_call(
        functools.partial(_attn_kernel, mode=mode, norm=norm, sink=sink, tq=tq, S=S, L=L),
        grid=(B, ATTN_KV_HEADS, nt),
        in_specs=in_specs,
        out_specs=pl.BlockSpec((tq, qw), o_map),
        out_shape=jax.ShapeDtypeStruct((n_rows, BRANCH_WIDTH), BF16),
        scratch_shapes=scratch,
        compiler_params=_cparams(("arbitrary", "arbitrary", "arbitrary")),
        name="attn_%s%s" % (mode, "_sink" if sink else ""),
    )(*args)


CONV_ROWS = 256
HALO = 16


def _gdn_conv_kernel(sfirst_ref, slast_ref, prev_ref, cur_ref, next_ref, w_ref, o_ref):
    r = pl.program_id(0)
    part = pl.program_id(1)
    n = cur_ref.shape[0]
    row = lax.broadcasted_iota(jnp.int32, (n, HEAD_DIM), 0)
    is_first = sfirst_ref[r] == 1
    is_last = slast_ref[r] == 1
    for h in range(GDN_HEADS):
        cols = slice(h * HEAD_DIM, (h + 1) * HEAD_DIM)
        x = cur_ref[:, cols].astype(F32)
        prev_row = jnp.where(is_first, 0.0, prev_ref[:, cols].astype(F32)[HALO - 1:HALO, :])
        next_row = jnp.where(is_last, 0.0, next_ref[:, cols].astype(F32)[0:1, :])
        xm = jnp.where(row == 0, prev_row, pltpu.roll(x, 1, 0))
        xp = jnp.where(row == n - 1, next_row, pltpu.roll(x, n - 1, 0))
        y = _silu(w_ref[0:1, cols] * xm + w_ref[1:2, cols] * x + w_ref[2:3, cols] * xp)
        rs = lax.rsqrt(jnp.sum(y * y, axis=-1, keepdims=True) + RMS_EPS)
        factor = jnp.where(part == 0, rs * (HEAD_DIM ** -0.5), jnp.where(part == 1, rs, 1.0))
        o_ref[:, cols] = y * factor


def _gdn_conv(z1, conv_w, seq_first, seq_last):
    M = z1.shape[0]
    nblk = M // CONV_ROWS
    W = GDN_HEADS * HEAD_DIM
    c0 = 3072 // W
    per = CONV_ROWS // HALO
    nh = M // HALO
    return pl.pallas_call(
        _gdn_conv_kernel,
        grid_spec=pltpu.PrefetchScalarGridSpec(
            num_scalar_prefetch=2, grid=(nblk, 3),
            in_specs=[
                pl.BlockSpec((HALO, W), lambda r, c, a, b: (jnp.maximum(r * per - 1, 0), c0 + c)),
                pl.BlockSpec((CONV_ROWS, W), lambda r, c, a, b: (r, c0 + c)),
                pl.BlockSpec((HALO, W), lambda r, c, a, b: (jnp.minimum((r + 1) * per, nh - 1), c0 + c)),
                pl.BlockSpec((3, W), lambda r, c, a, b: (0, c)),
            ],
            out_specs=pl.BlockSpec((CONV_ROWS, W), lambda r, c, a, b: (r, c))),
        out_shape=jax.ShapeDtypeStruct((M, 3 * W), F32),
        compiler_params=_cparams(("arbitrary", "arbitrary")),
        name="gdn_conv",
    )(seq_first, seq_last, z1, z1, z1, conv_w)


def _chunk_tri(n, upper):
    r = lax.broadcasted_iota(jnp.int32, (n, n), 0)
    c = lax.broadcasted_iota(jnp.int32, (n, n), 1)
    same = (r // CHUNK) == (c // CHUNK)
    tri = (c >= r) if upper else (c <= r)
    return jnp.where(same & tri, 1.0, 0.0).astype(BF16)


def _gdn_gates_kernel(zg_ref, a_ref, dtb_ref, o_ref):
    zg = zg_ref[...]
    n = zg.shape[0]
    lane = lax.broadcasted_iota(jnp.int32, zg.shape, 1)
    beta = _sigmoid(zg)
    t = zg + dtb_ref[...]
    softplus = jnp.maximum(t, 0.0) + jnp.log(1.0 + jnp.exp(-jnp.abs(t)))
    g = jnp.where((lane >= 16) & (lane < 32), a_ref[...] * softplus, 0.0)
    gc_f = _dot_exact_lhs(_chunk_tri(n, False), g)
    gc_b = _dot_exact_lhs(_chunk_tri(n, True), g)
    o_ref[...] = jnp.where(lane < 16, beta, jnp.where(lane < 24, gc_f, gc_b))


def _gdn_gates(zg, a_row, dtb_row):
    M = zg.shape[0]
    tm = 256
    return pl.pallas_call(
        _gdn_gates_kernel,
        grid=(M // tm,),
        in_specs=[pl.BlockSpec((tm, LANES), lambda i: (i, 0)),
                  pl.BlockSpec((1, LANES), lambda i: (0, 0)),
                  pl.BlockSpec((1, LANES), lambda i: (0, 0))],
        out_specs=pl.BlockSpec((tm, LANES), lambda i: (i, 0)),
        out_shape=jax.ShapeDtypeStruct((M, LANES), F32),
        compiler_params=_cparams(("arbitrary",)),
        name="gdn_gates",
    )(zg, a_row, dtb_row)


_INV_BASE = 8


def _split2(x):
    x1 = x.astype(BF16)
    return x1, (x - x1.astype(F32)).astype(BF16)


def _unit_tri_inverse(lms, r, c):
    C = lms[0].shape[0]
    eye = jnp.where(r == c, 1.0, 0.0)
    diag = (r // _INV_BASE) == (c // _INV_BASE)
    dms = [jnp.where(diag, lm, 0.0) for lm in lms]
    d16 = [dm.astype(BF16) for dm in dms]
    p16 = [_dot(d, d).astype(BF16) for d in d16]
    xs = [eye - dm for dm in dms]
    rrs = [_dot(jnp.concatenate([p, x.astype(BF16)], axis=0), p) for p, x in zip(p16, xs)]
    xs = [x + rr[C:] for x, rr in zip(xs, rrs)]
    ts = [x + _dot(x.astype(BF16), rr[:C].astype(BF16)) for x, rr in zip(xs, rrs)]
    s = _INV_BASE
    while s < C:
        join = ((r // (2 * s)) == (c // (2 * s))) & ((r // s) != (c // s))
        e16 = [jnp.where(join, lm, 0.0).astype(BF16) for lm in lms]
        t12 = [_split2(t) for t in ts]
        ya = [_dot(e, t1) for e, (t1, _) in zip(e16, t12)]
        yb = [_dot(e, t2) for e, (_, t2) in zip(e16, t12)]
        y12 = [_split2(a + b) for a, b in zip(ya, yb)]
        zs = [_dot(jnp.concatenate([t1, t2], axis=0), y1) for (t1, t2), (y1, _) in zip(t12, y12)]
        ws = [_dot(t1, y2) for (t1, _), (_, y2) in zip(t12, y12)]
        ts = [t - (z[:C] + z[C:] + w) for t, z, w in zip(ts, zs, ws)]
        s *= 2
    return ts


GDN_PREP_CHUNKS = 4


def _gdn_prep_kernel(q_ref, k_ref, v_ref, bf_ref, gf_ref, bb_ref, gb_ref, rf_ref, rb_ref,
                     wq_ref, u_ref, qkkd_ref, gl_ref):
    C = CHUNK
    r = lax.broadcasted_iota(jnp.int32, (C, C), 0)
    c = lax.broadcasted_iota(jnp.int32, (C, C), 1)
    gate_refs = ((bf_ref, gf_ref, rf_ref), (bb_ref, gb_ref, rb_ref))
    qs, ks, vs, kks, qks = [], [], [], [], []
    for n in range(GDN_PREP_CHUNKS):
        rows = slice(n * C, (n + 1) * C)
        qs.append(q_ref[rows, :])
        ks.append(k_ref[rows, :])
        vs.append(v_ref[rows, :])
        kb16 = ks[n].astype(BF16)
        kks.append(_dot_nt(kb16, kb16))
        qks.append(_dot_nt(qs[n].astype(BF16), kb16))
    units = [(n, d) for n in range(GDN_PREP_CHUNKS) for d in range(2)]
    lms, rhs1, rhs2 = [], [], []
    for n, d in units:
        beta_ref, gc_ref, grow_ref = gate_refs[d]
        rows = slice(n * C, (n + 1) * C)
        beta = beta_ref[0, rows, :]
        gc = gc_ref[0, rows, :]
        grow = grow_ref[0, n][0:1, :]
        incl = (c <= r) if d == 0 else (c >= r)
        strict = (c < r) if d == 0 else (c > r)
        last = C - 1 if d == 0 else 0
        decay = jnp.where(incl, jnp.exp(jnp.where(incl, gc[:, 0:C] - grow, 0.0)), 0.0)
        lms.append(jnp.where(strict, beta[:, 0:C] * kks[n] * decay, 0.0))
        eg = jnp.exp(gc)
        g_last = gc[last:last + 1, :]
        rhs = jnp.concatenate([ks[n] * beta * eg, vs[n] * beta], axis=-1)
        r1, r2 = _split2(rhs)
        rhs1.append(r1)
        rhs2.append(r2)
        wq_ref[d, (2 * n + 1) * C:(2 * n + 2) * C, :] = (qs[n] * eg).astype(BF16)
        kd = ks[n] * jnp.exp(g_last - gc)
        base = n * (C + HEAD_DIM)
        qkkd_ref[d, 0, base:base + C, :] = (qks[n] * decay).astype(BF16)
        qkkd_ref[d, 0, base + C:base + C + HEAD_DIM, :] = jnp.transpose(kd).astype(BF16)
        gl_ref[d, 0, n * 8:(n + 1) * 8, :] = jnp.broadcast_to(jnp.exp(g_last), (8, HEAD_DIM))
    t12 = [_split2(t) for t in _unit_tri_inverse(lms, r, c)]
    wa = [_dot(t1, r1) for (t1, _), r1 in zip(t12, rhs1)]
    wb = [_dot(t1, r2) for (t1, _), r2 in zip(t12, rhs2)]
    wc = [_dot(t2, r1) for (_, t2), r1 in zip(t12, rhs1)]
    for (n, d), a, b, cc in zip(units, wa, wb, wc):
        wu = a + b + cc
        wq_ref[d, 2 * n * C:(2 * n + 1) * C, :] = wu[:, :HEAD_DIM].astype(BF16)
        u_ref[d, n * C:(n + 1) * C, :] = wu[:, HEAD_DIM:]


def _gdn_prep(qkv, colb, rowb):
    M = qkv.shape[0]
    H = GDN_HEADS
    N = GDN_PREP_CHUNKS
    C = CHUNK
    nch = M // (C * N)
    cb_spec = lambda ch0: pl.BlockSpec((1, N * C, LANES), lambda c, h: (ch0 + h, c, 0))
    rb_spec = lambda ch0: pl.BlockSpec((1, N, 8, C), lambda c, h: (ch0 + h, c, 0, 0))
    return pl.pallas_call(
        _gdn_prep_kernel,
        grid=(nch, H),
        in_specs=[
            pl.BlockSpec((N * C, HEAD_DIM), lambda c, h: (c, h)),
            pl.BlockSpec((N * C, HEAD_DIM), lambda c, h: (c, H + h)),
            pl.BlockSpec((N * C, HEAD_DIM), lambda c, h: (c, 2 * H + h)),
            cb_spec(0), cb_spec(16), cb_spec(8), cb_spec(24),
            rb_spec(16), rb_spec(24),
        ],
        out_specs=[
            pl.BlockSpec((2, N * 2 * C, HEAD_DIM), lambda c, h: (0, c, h)),
            pl.BlockSpec((2, N * C, HEAD_DIM), lambda c, h: (0, c, h)),
            pl.BlockSpec((2, 1, N * (C + HEAD_DIM), C), lambda c, h: (0, h, c, 0)),
            pl.BlockSpec((2, 1, N * 8, HEAD_DIM), lambda c, h: (0, h, c, 0)),
        ],
        name="gdn_prep",
        out_shape=[
            jax.ShapeDtypeStruct((2, 2 * M, H * HEAD_DIM), BF16),
            jax.ShapeDtypeStruct((2, M, H * HEAD_DIM), F32),
            jax.ShapeDtypeStruct((2, H, (M // C) * (C + HEAD_DIM), C), BF16),
            jax.ShapeDtypeStruct((2, H, (M // C) * 8, HEAD_DIM), F32),
        ],
        compiler_params=_cparams(("arbitrary", "arbitrary")),
    )(qkv, qkv, qkv, colb, colb, colb, colb, rowb, rowb)


def _gdn_state_kernel(wqf_ref, uf_ref, qkf_ref, glf_ref, wqb_ref, ub_ref, qkb_ref, glb_ref,
                      of_ref, ob_ref, s_ref):
    C = CHUNK
    H = GDN_HEADS

    @pl.when(pl.program_id(1) == 0)
    def _():
        s_ref[...] = jnp.zeros(s_ref.shape, F32)

    dirs = ((wqf_ref, uf_ref, qkf_ref, glf_ref, of_ref), (wqb_ref, ub_ref, qkb_ref, glb_ref, ob_ref))
    units = [(d, h) for d in range(2) for h in range(H)]
    m1s = []
    for d, h in units:
        cols = slice(h * HEAD_DIM, (h + 1) * HEAD_DIM)
        m1s.append(_dot(dirs[d][0][0, :, cols], s_ref[d * H + h].astype(BF16)))
    m2s = []
    for (d, h), m1 in zip(units, m1s):
        cols = slice(h * HEAD_DIM, (h + 1) * HEAD_DIM)
        v_new = dirs[d][1][0, :, cols] - m1[0:C]
        m2s.append(_dot(dirs[d][2][0, h], v_new.astype(BF16)))
    for (d, h), m1, m2 in zip(units, m1s, m2s):
        cols = slice(h * HEAD_DIM, (h + 1) * HEAD_DIM)
        dirs[d][4][:, cols] = m1[C:2 * C] + m2[0:C]
        s_ref[d * H + h] = s_ref[d * H + h] * dirs[d][3][0, h][0:1, :] + m2[C:]


def _seq_chunk_maps(B, S, L):
    C = CHUNK
    lc, sc = L // C, S // C
    ctx0 = (B * S) // C

    def fwd(b, c):
        return jnp.where(c < lc, ctx0 + b * lc + c, b * sc + (c - lc))

    def bwd(b, c):
        return jnp.where(c < lc, ctx0 + b * lc + (lc - 1 - c), b * sc + (sc - 1 - (c - lc)))

    return fwd, bwd, lc + sc


def _gdn_state(wq, u, qkkd, gl, *, B, S, L):
    C = CHUNK
    H = GDN_HEADS
    M = u.shape[1]
    fwd, bwd, nc = _seq_chunk_maps(B, S, L)

    def specs(d, pos):
        return [
            pl.BlockSpec((1, 2 * C, H * HEAD_DIM), lambda b, c: (d, pos(b, c), 0)),
            pl.BlockSpec((1, C, H * HEAD_DIM), lambda b, c: (d, pos(b, c), 0)),
            pl.BlockSpec((1, H, C + HEAD_DIM, C), lambda b, c: (d, 0, pos(b, c), 0)),
            pl.BlockSpec((1, H, 8, HEAD_DIM), lambda b, c: (d, 0, pos(b, c), 0)),
        ]

    return pl.pallas_call(
        _gdn_state_kernel,
        grid=(B, nc),
        in_specs=specs(0, fwd) + specs(1, bwd),
        out_specs=[pl.BlockSpec((C, H * HEAD_DIM), lambda b, c: (fwd(b, c), 0)),
                   pl.BlockSpec((C, H * HEAD_DIM), lambda b, c: (bwd(b, c), 0))],
        out_shape=[jax.ShapeDtypeStruct((M, H * HEAD_DIM), F32)] * 2,
        scratch_shapes=[pltpu.VMEM((2 * H, HEAD_DIM, HEAD_DIM), F32)],
        compiler_params=_cparams(("arbitrary", "arbitrary")),
        name="gdn_state",
    )(wq, u, qkkd, gl, wq, u, qkkd, gl)


def _gla_kernel(qf_ref, kf_ref, vf_ref, zf_ref, qb_ref, kb_ref, vb_ref, zb_ref, wgk_ref, bgk_ref,
                of_ref, ob_ref, s_ref):
    C = CHUNK
    r = lax.broadcasted_iota(jnp.int32, (C, C), 0)
    c = lax.broadcasted_iota(jnp.int32, (C, C), 1)

    @pl.when(pl.program_id(1) == 0)
    def _():
        s_ref[...] = jnp.zeros(s_ref.shape, F32)

    dirs = ((qf_ref, kf_ref, vf_ref, zf_ref, of_ref), (qb_ref, kb_ref, vb_ref, zb_ref, ob_ref))
    incls = [(c <= r), (c >= r)]
    bcums = []
    for d in range(2):
        tri = jnp.where(incls[d], 1.0, 0.0).astype(BF16)
        pre = _dot_hi(dirs[d][3][...], wgk_ref[d]) + bgk_ref[d]
        gk = (jnp.minimum(pre, 0.0) - jnp.log(1.0 + jnp.exp(-jnp.abs(pre)))) * (1.0 / GLA_GATE_NORM)
        bcums.append(_dot_exact_lhs(tri, gk))
    units = [(d, h) for d in range(2) for h in range(GLA_HEADS)]
    scores, inter, upd, dls = [], [], [], []
    for d, h in units:
        kc = slice(h * GLA_DK, (h + 1) * GLA_DK)
        vc = slice(h * GLA_DV, (h + 1) * GLA_DV)
        mid = C // 2 - 1 if d == 1 else C // 2
        last = 0 if d == 1 else C - 1
        q = dirs[d][0][:, kc].astype(F32) * (GLA_DK ** -0.5)
        k = dirs[d][1][:, kc].astype(F32)
        v = dirs[d][2][:, vc]
        b = bcums[d][:, kc]
        b_mid = b[mid:mid + 1, :]
        b_last = b[last:last + 1, :]
        scores.append(_dot_nt((q * jnp.exp(b - b_mid)).astype(BF16), (k * jnp.exp(b_mid - b)).astype(BF16)))
        inter.append(_dot((q * jnp.exp(b)).astype(BF16), s_ref[d * GLA_HEADS + h].astype(BF16)))
        upd.append(_dot_tn((k * jnp.exp(b_last - b)).astype(BF16), v))
        dls.append(jnp.transpose(jnp.broadcast_to(jnp.exp(b_last), (GLA_DK, GLA_DK))))
    for (d, h), sc, oi, up, dl in zip(units, scores, inter, upd, dls):
        vc = slice(h * GLA_DV, (h + 1) * GLA_DV)
        p = jnp.where(incls[d], sc, 0.0).astype(BF16)
        dirs[d][4][:, vc] = _dot(p, dirs[d][2][:, vc]) + oi
        s_ref[d * GLA_HEADS + h] = s_ref[d * GLA_HEADS + h] * jnp.concatenate([dl, dl], axis=-1) + up


def _gla(z2, zg, wgk_pad, bgk, *, B, S, L):
    C = CHUNK
    M = z2.shape[0]
    fwd, bwd, nc = _seq_chunk_maps(B, S, L)
    kw = GLA_HEADS * GLA_DK
    vw = GLA_HEADS * GLA_DV

    def specs(pos):
        return [
            pl.BlockSpec((C, kw), lambda b, c: (pos(b, c), 0)),
            pl.BlockSpec((C, kw), lambda b, c: (pos(b, c), 1)),
            pl.BlockSpec((C, vw), lambda b, c: (pos(b, c), 1)),
            pl.BlockSpec((C, LANES), lambda b, c: (pos(b, c), 0)),
        ]

    return pl.pallas_call(
        _gla_kernel,
        grid=(B, nc),
        in_specs=specs(fwd) + specs(bwd) + [
            pl.BlockSpec((2, LANES, kw), lambda b, c: (0, 0, 0)),
            pl.BlockSpec((2, 1, kw), lambda b, c: (0, 0, 0)),
        ],
        out_specs=[pl.BlockSpec((C, vw), lambda b, c: (fwd(b, c), 0)),
                   pl.BlockSpec((C, vw), lambda b, c: (bwd(b, c), 0))],
        out_shape=[jax.ShapeDtypeStruct((M, vw), F32)] * 2,
        scratch_shapes=[pltpu.VMEM((2 * GLA_HEADS, GLA_DK, GLA_DV), F32)],
        compiler_params=_cparams(("arbitrary", "arbitrary")),
        name="gla",
    )(z2, z2, z2, zg, z2, z2, z2, zg, wgk_pad, bgk)


def _headnorm_kernel(of_ref, ob_ref, z_ref, g_ref, o_ref, *, hd):
    n = of_ref.shape[1] // hd
    for h in range(n):
        cols = slice(h * hd, (h + 1) * hd)
        o = of_ref[:, cols] + ob_ref[:, cols]
        y = o * lax.rsqrt(jnp.mean(o * o, axis=-1, keepdims=True) + RMS_EPS) * g_ref[...]
        o_ref[:, cols] = (y * _silu(z_ref[:, cols].astype(F32))).astype(o_ref.dtype)


def _headnorm(of, ob, z, zcol, gain, *, hd, tm=256):
    M, W = of.shape
    return pl.pallas_call(
        functools.partial(_headnorm_kernel, hd=hd),
        grid=(M // tm,),
        in_specs=[pl.BlockSpec((tm, W), lambda i: (i, 0)),
                  pl.BlockSpec((tm, W), lambda i: (i, 0)),
                  pl.BlockSpec((tm, W), lambda i: (i, zcol)),
                  pl.BlockSpec((1, hd), lambda i: (0, 0))],
        out_specs=pl.BlockSpec((tm, W), lambda i: (i, 0)),
        out_shape=jax.ShapeDtypeStruct((M, W), BF16),
        compiler_params=_cparams(("arbitrary",)),
        name="headnorm",
    )(of, ob, z, gain.reshape(1, hd))


def _final_kernel(rowid_ref, *refs, with_y, norm):
    del rowid_ref
    refs = list(refs)
    x = refs.pop(0)[...]
    if with_y:
        y0_ref, y1_ref, gt_ref = refs[:3]
        refs = refs[3:]
        x = x + gt_ref[0] * (y0_ref[...] + y1_ref[...])
    if norm:
        g_ref = refs.pop(0)
        x = x * lax.rsqrt(jnp.mean(x * x, axis=-1, keepdims=True) + RMS_EPS) * g_ref[...]
    refs[0][...] = x


def _final(X, ys, mod3, rowid, layer, k_gate, g_final, *, tm, grid_m):
    D = X.shape[1]
    row_spec = pl.BlockSpec((tm, D), lambda i, r: (i, 0))
    args, in_specs = [X], [row_spec]
    if ys is not None:
        args += [ys[0], ys[1], mod3]
        in_specs += [row_spec, row_spec,
                     pl.BlockSpec((1, 1, D), lambda i, r: (layer * 16 + r[i], 0, k_gate))]
    if g_final is not None:
        args.append(g_final.reshape(1, D))
        in_specs.append(pl.BlockSpec((1, D), lambda i, r: (0, 0)))
    return pl.pallas_call(
        functools.partial(_final_kernel, with_y=ys is not None, norm=g_final is not None),
        grid_spec=pltpu.PrefetchScalarGridSpec(
            num_scalar_prefetch=1, grid=(grid_m,), in_specs=in_specs, out_specs=row_spec),
        out_shape=jax.ShapeDtypeStruct((grid_m * tm, D), F32),
        compiler_params=_cparams(("arbitrary",)),
        name="final",
    )(rowid, *args)


def _rope_tables(S):
    rows = S // GRID_W
    row = jnp.repeat(jnp.arange(rows, dtype=F32), GRID_W)
    col = jnp.tile(jnp.arange(GRID_W, dtype=F32), rows)
    inv_freq = ROPE_THETA ** (-jnp.arange(ROPE_FREQS, dtype=F32) / ROPE_FREQS)
    ar = row[:, None] * inv_freq
    ac = col[:, None] * inv_freq
    cos = jnp.concatenate([jnp.cos(ar), jnp.cos(ar), jnp.cos(ac), jnp.cos(ac)], axis=-1)
    sin = jnp.concatenate([-jnp.sin(ar), jnp.sin(ar), -jnp.sin(ac), jnp.sin(ac)], axis=-1)
    return cos, sin


def _stream_rowid(B, S, L, tm, grid_m):
    i = jnp.arange(grid_m, dtype=jnp.int32) * tm
    return jnp.where(i < B * S, i // S, B).astype(jnp.int32)


def _moe_plan(logits, T, tm, n_tiles):
    top_val, top_idx = lax.top_k(logits, TOP_K)
    top_w = jax.nn.softmax(top_val, axis=-1)
    e_flat = top_idx.reshape(-1).astype(jnp.int32)
    w_flat = top_w.reshape(-1)
    order = jnp.argsort(e_flat, stable=True)
    e_sorted = e_flat[order]
    sizes = jnp.bincount(e_flat, length=N_EXPERTS).astype(jnp.int32)
    start = jnp.cumsum(sizes) - sizes
    padded = ((sizes + tm - 1) // tm) * tm
    pend = jnp.cumsum(padded)
    pstart = pend - padded
    rank = jnp.arange(T * TOP_K, dtype=jnp.int32) - start[e_sorted]
    dest_sorted = pstart[e_sorted] + rank
    P = n_tiles * tm
    row_token = jnp.zeros((P,), jnp.int32).at[dest_sorted].set((order // TOP_K).astype(jnp.int32))
    row_w = jnp.zeros((P,), F32).at[dest_sorted].set(w_flat[order])
    dest = jnp.zeros((T * TOP_K,), jnp.int32).at[order].set(dest_sorted)
    tile0 = jnp.arange(n_tiles, dtype=jnp.int32) * tm
    tile_e = jnp.minimum(jnp.searchsorted(pend, tile0, side='right'), N_EXPERTS - 1).astype(jnp.int32)
    valid = (tile0 < pend[-1]).astype(jnp.int32)
    last_e = tile_e[jnp.maximum(pend[-1] // tm - 1, 0)]
    tile_e = jnp.where(valid == 1, tile_e, last_e)
    first = jnp.concatenate([jnp.ones((1,), jnp.int32), (tile_e[1:] != tile_e[:-1]).astype(jnp.int32)])
    return row_token, row_w, dest.reshape(T, TOP_K), tile_e, first, valid


def kernel(x, c, ctx, c_ctx, w_ada, b_ada, g_norm_mix, g_norm_ffn, w_in, a_q_gain, a_k_gain, b_sink,
           gdn_conv, gdn_a_log, gdn_dt_bias, gdn_norm_gain, gla_w_gk, gla_b_gk, gla_norm_gain,
           w_branch, w_merge, w_out, w1_dense, w3_dense, w2_dense, w_router, w1_moe, w3_moe, w2_moe,
           g_final):
    B, S, D = x.shape
    L = ctx.shape[1]
    depth = w_in.shape[0]
    n_lat = B * S
    M = n_lat + B * L
    TM = 512
    assert D == D_MODEL and S % CONV_ROWS == 0 and L % CONV_ROWS == 0 and n_lat % TM == 0
    assert (B * L) % TM == 0 and S % TM == 0 and S >= 3 * Q_BLOCK and n_lat % L == 0 and B <= 15

    cond = jnp.zeros((16, D), F32).at[:B].set(c).at[B].set(c_ctx)
    cond2 = jnp.concatenate([cond] * depth, axis=0)
    (mod,) = _panel_call(
        _ep_ada, name="ada", grid_n=(6 * D) // 512, grid_m=depth,
        lhs=[(cond2, (16, D), lambda j, i, e, f, v, r: (i, 0))],
        weights=[(w_ada, (1, D, 512), lambda j, i, e, f, v, r: (e[i], 0, j))],
        extras=[(b_ada.reshape(depth, 1, 6 * D), (1, 1, 512), lambda j, i, e, f, v, r: (e[i], 0, j))],
        outs=[(jax.ShapeDtypeStruct((16 * depth, 6 * D), F32), (16, 512), lambda j, i, e, f, v, r: (i, j))],
        eid=jnp.arange(depth, dtype=jnp.int32), first=jnp.ones((depth,), jnp.int32))
    mod3 = mod.reshape(16 * depth, 1, 6 * D)

    cos, sin = _rope_tables(S)
    X = jnp.concatenate([x.reshape(n_lat, D), ctx.reshape(B * L, D)], axis=0)
    gm_all = M // TM
    gm_lat = n_lat // TM
    rowid = _stream_rowid(B, S, L, TM, gm_all)
    rowid256 = _stream_rowid(B, S, L, 256, M // 256)

    blk = jnp.arange(M // CONV_ROWS, dtype=jnp.int32) * CONV_ROWS
    seq_len = jnp.where(blk < n_lat, S, L)
    seq_off = jnp.where(blk < n_lat, blk, blk - n_lat)
    seq_first = (seq_off % seq_len == 0).astype(jnp.int32)
    seq_last = ((seq_off + CONV_ROWS) % seq_len == 0).astype(jnp.int32)

    out = None
    for layer in range(depth):
        last = layer == depth - 1
        gm = gm_lat if last else gm_all
        h = _modulate(X, g_norm_mix, mod3, rowid256, layer, 0, 1, tm=256, grid_m=M // 256)
        z1 = _matmul_stream(h, w_in, layer, 0, Z1_WIDTH, name="proj_z1", tm=TM, tn=1024, out_dtype=BF16)
        w_d = w_in[layer][:, COL_D_Q:COL_D_GATE]
        z2 = _matmul_stream(h, w_d, layer, 0, Z2_WIDTH, name="proj_z2", tm=TM, tn=1024, out_dtype=BF16)
        w_g = jnp.concatenate([w_in[layer][:, COL_C_BETA:COL_D_Q], w_in[layer][:, COL_D_GATE:],
                               jnp.zeros((D, LANES - 64), F32)], axis=1)
        zg = _matmul_stream(h, w_g, layer, 0, LANES, name="proj_gates", tm=TM, tn=LANES, out_dtype=F32)

        oa = _attention(z1, cos, sin, a_q_gain[layer], a_k_gain[layer], None, mode='all',
                        B=B, S=S, L=L, qcol=0, kcol=8, vcol=10)
        ob = _attention(z1, cos, sin, None, None, b_sink[layer], mode='band',
                        B=B, S=S, L=L, qcol=3, kcol=20, vcol=22)
        if not last:
            oa_c = _attention(z1, cos, sin, a_q_gain[layer], a_k_gain[layer], None, mode='ctx',
                              B=B, S=S, L=L, qcol=0, kcol=8, vcol=10)
            ob_c = _attention(z1, cos, sin, None, None, b_sink[layer], mode='ctx',
                              B=B, S=S, L=L, qcol=3, kcol=20, vcol=22)
            oa = jnp.concatenate([oa, oa_c], axis=0)
            ob = jnp.concatenate([ob, ob_c], axis=0)

        qkv = _gdn_conv(z1, gdn_conv[layer], seq_first, seq_last)
        a_row = jnp.zeros((1, LANES), F32).at[0, 16:32].set(-jnp.exp(gdn_a_log[layer]).reshape(-1))
        dtb_row = jnp.zeros((1, LANES), F32).at[0, 16:32].set(gdn_dt_bias[layer].reshape(-1))
        gates = _gdn_gates(zg, a_row, dtb_row)
        gt = gates[:, :32].T
        colb = jnp.broadcast_to(gt[:, :, None], (32, M, LANES))
        rowb = jnp.broadcast_to(gt.reshape(32, M // CHUNK, 1, CHUNK), (32, M // CHUNK, 8, CHUNK))
        wq, u, qkkd, gl = _gdn_prep(qkv, colb, rowb)
        oc_f, oc_b = _gdn_state(wq, u, qkkd, gl, B=B, S=S, L=L)
        oc = _headnorm(oc_f, oc_b, z1, 6144 // BRANCH_WIDTH, gdn_norm_gain[layer], hd=HEAD_DIM)

        wgk_pad = jnp.zeros((2, LANES, GLA_HEADS * GLA_DK), F32)
        wgk_pad = wgk_pad.at[0, 32:48].set(gla_w_gk[layer, 0]).at[1, 48:64].set(gla_w_gk[layer, 1])
        od_f, od_b = _gla(z2, zg, wgk_pad, gla_b_gk[layer].reshape(2, 1, -1), B=B, S=S, L=L)
        od = _headnorm(od_f, od_b, z2, 2048 // BRANCH_WIDTH, gla_norm_gain[layer], hd=GLA_DV)

        tn_m = 256
        branches = [oa, ob, oc, od]
        lhs = [(h, (TM, D), lambda j, i, e, f, v, r: (i, 0))]
        lhs += [(o, (TM, BRANCH_WIDTH), lambda j, i, e, f, v, r: (i, 0)) for o in branches]
        wts = [(w_merge, (1, D, tn_m), lambda j, i, e, f, v, r, n=n: (layer, 0, n * (D // tn_m) + j))
               for n in range(N_BRANCH)]
        wts += [(w_branch, (1, 1, BRANCH_WIDTH, tn_m), lambda j, i, e, f, v, r, n=n: (layer, n, 0, j))
                for n in range(N_BRANCH)]
        (merged,) = _panel_call(
            _ep_merge, name="merge", grid_n=D // tn_m, grid_m=gm, lhs=lhs, weights=wts, extras=[],
            outs=[(jax.ShapeDtypeStruct((gm * TM, D), BF16), (TM, tn_m), lambda j, i, e, f, v, r: (i, j))])

        def resid(name, xin, w, wmap, kdim, k_gate, tn):
            (res,) = _panel_call(
                _ep_resid, name=name, grid_n=D // tn, grid_m=gm,
                lhs=[(xin, (TM, kdim), lambda j, i, e, f, v, r: (i, 0))],
                weights=[(w, (1, kdim, tn), wmap)],
                extras=[(X, (TM, tn), lambda j, i, e, f, v, r: (i, j)),
                        (mod3, (1, 1, tn), lambda j, i, e, f, v, r: (layer * 16 + r[i], 0, k_gate * (D // tn) + j))],
                outs=[(jax.ShapeDtypeStruct((gm * TM, D), F32), (TM, tn), lambda j, i, e, f, v, r: (i, j))],
                rowid=rowid[:gm])
            return res

        X = resid("out_proj", merged, w_out,lambda j, i, e, f, v, r: (layer, 0, j), D, 2, 512)

        if layer % 2 == 0:
            li = layer // 2
            h2 = _modulate(X, g_norm_ffn, mod3, rowid256, layer, 3, 4, tm=256, grid_m=(gm * TM) // 256)
            d_ff = w1_dense.shape[2]
            (uu,) = _panel_call(
                _ep_swiglu, name="dense_up", grid_n=d_ff // 512, grid_m=gm,
                lhs=[(h2, (TM, D), lambda j, i, e, f, v, r: (i, 0))],
                weights=[(w1_dense, (1, D, 512), lambda j, i, e, f, v, r: (li, 0, j)),
                         (w3_dense, (1, D, 512), lambda j, i, e, f, v, r: (li, 0, j))],
                extras=[],
                outs=[(jax.ShapeDtypeStruct((gm * TM, d_ff), BF16), (TM, 512), lambda j, i, e, f, v, r: (i, j))])
            X = resid("dense_down", uu, w2_dense, lambda j, i, e, f, v, r: (li, 0, j), d_ff, 5, 512)
            if last:
                out = _final(X, None, mod3, rowid256, layer, 5, g_final, tm=256, grid_m=n_lat // 256)
        else:
            li = layer // 2
            T = gm * TM
            wr_pad = jnp.zeros((D, LANES), F32).at[:, :N_EXPERTS].set(w_router[li])
            h2, logits = _modulate(X, g_norm_ffn, mod3, rowid256, layer, 3, 4, tm=256,
                                   grid_m=T // 256, w_router=wr_pad)
            tm_e = 512
            n_tiles = (T * TOP_K) // tm_e + N_EXPERTS
            row_token, row_w, dest, tile_e, first, valid = _moe_plan(logits[:, :N_EXPERTS], T, tm_e, n_tiles)
            xg = jnp.take(h2, row_token, axis=0)
            d_ffe = w1_moe.shape[3]
            (uu,) = _panel_call(
                _ep_swiglu, name="moe_up", grid_n=d_ffe // 1024, grid_m=n_tiles,
                lhs=[(xg, (tm_e, D), lambda j, i, e, f, v, r: (i, 0))],
                weights=[(w1_moe, (1, 1, D, 1024), lambda j, i, e, f, v, r: (li, e[i], 0, j)),
                         (w3_moe, (1, 1, D, 1024), lambda j, i, e, f, v, r: (li, e[i], 0, j))],
                extras=[],
                outs=[(jax.ShapeDtypeStruct((n_tiles * tm_e, d_ffe), BF16), (tm_e, 1024),
                       lambda j, i, e, f, v, r: (i, j))],
                eid=tile_e, first=first, valid=valid)
            tm_d = 256
            tn_d = 512
            rep = tm_e // tm_d
            first_d = jnp.repeat(first, rep) * (jnp.arange(n_tiles * rep, dtype=jnp.int32) % rep == 0)
            (yo,) = _panel_call(
                _ep_rowscale, name="moe_down", grid_n=D // tn_d, grid_m=n_tiles * rep,
                lhs=[(uu, (tm_d, d_ffe), lambda j, i, e, f, v, r: (i, 0))],
                weights=[(w2_moe, (1, 1, d_ffe, tn_d), lambda j, i, e, f, v, r: (li, e[i], 0, j))],
                extras=[(row_w.reshape(-1, 1), (tm_d, 1), lambda j, i, e, f, v, r: (i, 0))],
                outs=[(jax.ShapeDtypeStruct((n_tiles * tm_e, D), F32), (tm_d, tn_d),
                       lambda j, i, e, f, v, r: (i, j))],
                eid=jnp.repeat(tile_e, rep), first=first_d.astype(jnp.int32), valid=jnp.repeat(valid, rep))
            ys = (jnp.take(yo, dest[:, 0], axis=0), jnp.take(yo, dest[:, 1], axis=0))
            res = _final(X, ys, mod3, rowid256, layer, 5, g_final if last else None, tm=256,
                         grid_m=T // 256)
            if last:
                out = res
            else:
                X = res
    return out.reshape(B, S, D)
```

```python
import functools

import jax
import jax.numpy as jnp
from jax import lax
from jax.experimental import pallas as pl
from jax.experimental.pallas import tpu as pltpu

F32 = jnp.float32
BF16 = jnp.bfloat16

D_MODEL = 2048
HEAD_DIM = 128
GRID_W = 64
ROPE_THETA = 10000.0
ROPE_FREQS = HEAD_DIM // 4
RMS_EPS = 1e-6
BRANCH_WIDTH = D_MODEL // 2
KV_GROUP = 4
ATTN_KV_HEADS = 2
WINDOW = 128
Q_BLOCK = 128
GDN_HEADS = 8
GLA_HEADS = 4
GLA_DK = 128
GLA_DV = 256
GLA_RANK = 16
GLA_GATE_NORM = 16.0
CHUNK = 64
N_EXPERTS = 8
TOP_K = 2
N_BRANCH = 4
NEG_BIG = -1e30
LOG2E = 1.4426950408889634

COL_C_BETA = 7168
COL_D_Q = 7200
COL_D_GATE = 10272
PROJ_WIDTH = 10304
Z1_WIDTH = 7168
Z2_WIDTH = 3072

VMEM_LIMIT = 56 * 1024 * 1024
LANES = 128


def _cparams(sem):
    return pltpu.CompilerParams(dimension_semantics=sem, vmem_limit_bytes=VMEM_LIMIT)


def _silu(x):
    return x * (1.0 / (1.0 + jnp.exp(-x)))


def _sigmoid(x):
    return 1.0 / (1.0 + jnp.exp(-x))


def _split3(x):
    x1 = x.astype(BF16)
    r1 = x - x1.astype(F32)
    x2 = r1.astype(BF16)
    x3 = (r1 - x2.astype(F32)).astype(BF16)
    return x1, x2, x3


def _dot(a, b):
    return jnp.dot(a, b, preferred_element_type=F32)


def _dot_nt(a, b):
    return lax.dot_general(a, b, (((1,), (1,)), ((), ())), preferred_element_type=F32)


def _dot_tn(a, b):
    return lax.dot_general(a, b, (((0,), (0,)), ((), ())), preferred_element_type=F32)


def _dot_exact_lhs(tri_bf16, x):
    x1, x2, x3 = _split3(x)
    return _dot(tri_bf16, x1) + _dot(tri_bf16, x2) + _dot(tri_bf16, x3)


def _dot_hi(a, b):
    a1 = a.astype(BF16)
    a2 = (a - a1.astype(F32)).astype(BF16)
    b1 = b.astype(BF16)
    b2 = (b - b1.astype(F32)).astype(BF16)
    return _dot(a1, b1) + _dot(a1, b2) + _dot(a2, b1)


_CAST_ROWS = 256


def _panel_kernel(eid_ref, first_ref, valid_ref, rowid_ref, *refs, n_lhs, n_w, n_ex, n_out,
                  w_rows, epilogue):
    del eid_ref, rowid_ref
    lhs = refs[:n_lhs]
    ws = refs[n_lhs:n_lhs + n_w]
    exs = refs[n_lhs + n_w:n_lhs + n_w + n_ex]
    outs = refs[n_lhs + n_w + n_ex:n_lhs + n_w + n_ex + n_out]
    wbs = refs[n_lhs + n_w + n_ex + n_out:]
    i = pl.program_id(1)

    @pl.when(first_ref[i] == 1)
    def _():
        for w_ref, wb_ref, rows in zip(ws, wbs, w_rows):
            lead = (0,) * (len(w_ref.shape) - 2)

            def cast(r, carry, w_ref=w_ref, wb_ref=wb_ref, lead=lead):
                rr = pl.multiple_of(r * _CAST_ROWS, _CAST_ROWS)
                wb_ref[pl.ds(rr, _CAST_ROWS), :] = w_ref[lead + (pl.ds(rr, _CAST_ROWS), slice(None))].astype(BF16)
                return carry

            lax.fori_loop(0, rows // _CAST_ROWS, cast, 0)

    @pl.when(valid_ref[i] == 1)
    def _():
        epilogue(lhs, wbs, exs, outs)

    @pl.when(valid_ref[i] == 0)
    def _():
        for o in outs:
            o[...] = jnp.zeros(o.shape, o.dtype)


def _panel_call(epilogue, *, name, grid_n, grid_m, lhs, weights, extras, outs, eid=None, first=None,
                valid=None, rowid=None):
    if eid is None:
        eid = jnp.zeros((grid_m,), jnp.int32)
    if first is None:
        first = jnp.zeros((grid_m,), jnp.int32).at[0].set(1)
    if valid is None:
        valid = jnp.ones((grid_m,), jnp.int32)
    if rowid is None:
        rowid = jnp.zeros((grid_m,), jnp.int32)
    in_arrays, in_specs = [], []
    for arr, blk, imap in list(lhs) + list(weights) + list(extras):
        in_arrays.append(arr)
        in_specs.append(pl.BlockSpec(blk, imap))
    out_shapes = [o[0] for o in outs]
    out_specs = [pl.BlockSpec(o[1], o[2]) for o in outs]
    w_rows = [blk[-2] for _, blk, _ in weights]
    scratch = [pltpu.VMEM((blk[-2], blk[-1]), BF16) for _, blk, _ in weights]
    kern = functools.partial(_panel_kernel, n_lhs=len(lhs), n_w=len(weights), n_ex=len(extras),
                             n_out=len(outs), w_rows=w_rows, epilogue=epilogue)
    res = pl.pallas_call(
        kern,
        grid_spec=pltpu.PrefetchScalarGridSpec(
            num_scalar_prefetch=4, grid=(grid_n, grid_m),
            in_specs=in_specs, out_specs=out_specs, scratch_shapes=scratch),
        out_shape=out_shapes,
        compiler_params=_cparams(("arbitrary", "arbitrary")),
        name=name,
    )(eid, first, valid, rowid, *in_arrays)
    return res


def _ep_plain(lhs, wbs, exs, outs):
    outs[0][...] = _dot(lhs[0][...], wbs[0][...]).astype(outs[0].dtype)


def _ep_ada(lhs, wbs, exs, outs):
    x = _silu(lhs[0][...]).astype(BF16)
    outs[0][...] = _dot(x, wbs[0][...]) + exs[0][0]


def _ep_swiglu(lhs, wbs, exs, outs):
    x = lhs[0][...]
    a1 = _dot(x, wbs[0][...])
    a3 = _dot(x, wbs[1][...])
    outs[0][...] = (_silu(a1) * a3).astype(outs[0].dtype)


def _ep_resid(lhs, wbs, exs, outs):
    outs[0][...] = exs[0][...] + exs[1][0] * _dot(lhs[0][...], wbs[0][...])


def _ep_rowscale(lhs, wbs, exs, outs):
    outs[0][...] = (exs[0][...] * _dot(lhs[0][...], wbs[0][...])).astype(outs[0].dtype)


def _ep_merge(lhs, wbs, exs, outs):
    h = lhs[0][...]
    acc = None
    for n in range(N_BRANCH):
        gate = _sigmoid(_dot(h, wbs[n][...]))
        y = _dot(lhs[1 + n][...], wbs[N_BRANCH + n][...])
        acc = gate * y if acc is None else acc + gate * y
    outs[0][...] = acc.astype(outs[0].dtype)


def _matmul_stream(x, w, layer, col0, n_cols, *, name, tm, tn, out_dtype, grid_m=None):
    M, K = x.shape
    grid_m = M // tm if grid_m is None else grid_m
    jb = col0 // tn
    if w.ndim == 3:
        wspec = (w, (1, K, tn), lambda j, i, e, f, v, r: (layer, 0, jb + j))
    else:
        wspec = (w, (K, tn), lambda j, i, e, f, v, r: (0, jb + j))
    (out,) = _panel_call(
        _ep_plain, name=name, grid_n=n_cols // tn, grid_m=grid_m,
        lhs=[(x, (tm, K), lambda j, i, e, f, v, r: (i, 0))],
        weights=[wspec], extras=[],
        outs=[(jax.ShapeDtypeStruct((grid_m * tm, n_cols), out_dtype), (tm, tn),
               lambda j, i, e, f, v, r: (i, j))])
    return out


def _modulate_kernel(rowid_ref, x_ref, g_ref, sh_ref, sc_ref, *rest, with_router):
    del rowid_ref
    x = x_ref[...]
    y = x * lax.rsqrt(jnp.mean(x * x, axis=-1, keepdims=True) + RMS_EPS)
    h = (y * g_ref[0]) * (1.0 + sc_ref[0]) + sh_ref[0]
    if with_router:
        wr_ref, h_ref, lg_ref = rest
        h_ref[...] = h.astype(BF16)
        lg_ref[...] = _dot_hi(h, wr_ref[...])
    else:
        (h_ref,) = rest
        h_ref[...] = h.astype(BF16)


def _modulate(X, gain, mod3, rowid, layer, k_shift, k_scale, *, tm, grid_m, w_router=None):
    D = X.shape[1]
    with_router = w_router is not None
    in_specs = [
        pl.BlockSpec((tm, D), lambda i, r: (i, 0)),
        pl.BlockSpec((1, 1, D), lambda i, r: (layer, 0, 0)),
        pl.BlockSpec((1, 1, D), lambda i, r: (layer * 16 + r[i], 0, k_shift)),
        pl.BlockSpec((1, 1, D), lambda i, r: (layer * 16 + r[i], 0, k_scale)),
    ]
    args = [X, gain.reshape(gain.shape[0], 1, D), mod3, mod3]
    out_shape = [jax.ShapeDtypeStruct((grid_m * tm, D), BF16)]
    out_specs = [pl.BlockSpec((tm, D), lambda i, r: (i, 0))]
    if with_router:
        in_specs.append(pl.BlockSpec((D, LANES), lambda i, r: (0, 0)))
        args.append(w_router)
        out_shape.append(jax.ShapeDtypeStruct((grid_m * tm, LANES), F32))
        out_specs.append(pl.BlockSpec((tm, LANES), lambda i, r: (i, 0)))
    res = pl.pallas_call(
        functools.partial(_modulate_kernel, with_router=with_router),
        grid_spec=pltpu.PrefetchScalarGridSpec(
            num_scalar_prefetch=1, grid=(grid_m,), in_specs=in_specs, out_specs=out_specs),
        out_shape=out_shape,
        compiler_params=_cparams(("arbitrary",)),
        name="modulate_router" if with_router else "modulate",
    )(rowid, *args)
    return res if with_router else res[0]


def _swap_halves(x):
    lane = lax.broadcasted_iota(jnp.int32, x.shape, x.ndim - 1)
    return jnp.where((lane % 64) < 32, pltpu.roll(x, 96, x.ndim - 1), pltpu.roll(x, 32, x.ndim - 1))


def _rope(x, cos, sin_signed):
    return x * cos + _swap_halves(x) * sin_signed


def _head_rms(x, gain_row):
    return x * lax.rsqrt(jnp.mean(x * x, axis=-1, keepdims=True) + RMS_EPS) * gain_row


ATTN_ALL_ROWS = 256


def _lane_tile_max(s):
    m = s[:, 0:LANES]
    for k in range(1, s.shape[1] // LANES):
        m = jnp.maximum(m, s[:, k * LANES:(k + 1) * LANES])
    return m


def _attn_kernel(*refs, mode, norm, sink, tq, S, L):
    refs = list(refs)
    sink_ref = refs.pop(0) if sink else None
    q_ref, kc_ref, vc_ref = refs[:3]
    refs = refs[3:]
    if mode != 'ctx':
        kl_ref, vl_ref, cos_ref, sin_ref = refs[:4]
        refs = refs[4:]
    if norm:
        qg_ref, kg_ref = refs[:2]
        refs = refs[2:]
    o_ref, qs_ref = refs[:2]
    kp_ref = refs[2] if len(refs) > 2 else None
    ve_ref = refs[3] if len(refs) > 3 else None
    g = pl.program_id(1)
    t = pl.program_id(2)
    rows = 256

    if kp_ref is not None:
        @pl.when(t == 0)
        def _():
            if mode != 'band':
                for r0 in range(0, L, rows):
                    kc = kc_ref[r0:r0 + rows, :].astype(F32)
                    if norm:
                        kc = _head_rms(kc, kg_ref[...])
                    kp_ref[r0:r0 + rows, :] = kc.astype(BF16)
            if mode != 'ctx':
                base = L if mode == 'all' else 0
                for r0 in range(0, S, rows):
                    kl = kl_ref[r0:r0 + rows, :].astype(F32)
                    if norm:
                        kl = _head_rms(kl, kg_ref[...])
                    kl = _rope(kl, cos_ref[r0:r0 + rows, :], sin_ref[r0:r0 + rows, :])
                    kp_ref[base + r0:base + r0 + rows, :] = kl.astype(BF16)
                ve_ref[0:L, 0:HEAD_DIM] = vc_ref[...]
                ve_ref[L:L + S, 0:HEAD_DIM] = vl_ref[...]
                ve_ref[:, HEAD_DIM:2 * HEAD_DIM] = jnp.ones((L + S, HEAD_DIM), BF16)

    scale = (HEAD_DIM ** -0.5) * LOG2E
    if mode != 'ctx':
        t0 = pl.multiple_of(t * tq, tq)
        cos_q = cos_ref[pl.ds(t0, tq), :]
        sin_q = sin_ref[pl.ds(t0, tq), :]
    if mode == 'band':
        nb = 3 * Q_BLOCK
        start = pl.multiple_of(jnp.clip(t * tq - WINDOW, 0, S - nb), Q_BLOCK)
        qpos = t * tq + lax.broadcasted_iota(jnp.int32, (tq, nb), 0)
        kpos = start + lax.broadcasted_iota(jnp.int32, (tq, nb), 1)
        in_band = jnp.abs(qpos - kpos) <= WINDOW

    def scores(j):
        qj = q_ref[:, j * HEAD_DIM:(j + 1) * HEAD_DIM].astype(F32)
        if norm:
            qj = _head_rms(qj, qg_ref[...])
        if mode != 'ctx':
            qj = _rope(qj, cos_q, sin_q)
        qs_ref[j * tq:(j + 1) * tq, :] = (qj * scale).astype(BF16)
        qj = qs_ref[j * tq:(j + 1) * tq, :]
        if mode == 'all':
            half = (L + S) // 2
            return _dot_nt(qj, kp_ref[0:half, :]), _dot_nt(qj, kp_ref[half:L + S, :])
        if mode == 'band':
            s_loc = _dot_nt(qj, kp_ref[pl.ds(start, nb), :])
            return _dot_nt(qj, kc_ref[...]), jnp.where(in_band, s_loc, NEG_BIG)
        return (_dot_nt(qj, kp_ref[...] if norm else kc_ref[...]),)

    def finish(j, sc):
        sink_j = sink_ref[g * KV_GROUP + j] * LOG2E if sink else None
        if mode == 'ctx':
            (s,) = sc
            m = jnp.max(s, axis=-1, keepdims=True)
            if sink:
                m = jnp.maximum(m, sink_j)
            p = jnp.exp2(s - m)
            l = jnp.sum(p, axis=-1, keepdims=True)
            if sink:
                l = l + jnp.exp2(sink_j - m)
            o = _dot(p.astype(BF16), vc_ref[...]) / l
        else:
            s_a, s_b = sc
            m = jnp.max(jnp.maximum(_lane_tile_max(s_a), _lane_tile_max(s_b)), axis=-1, keepdims=True)
            if sink:
                m = jnp.maximum(m, sink_j)
            p_a = jnp.exp2((s_a - m).astype(BF16))
            p_b = jnp.exp2((s_b - m).astype(BF16))
            if mode == 'all':
                half = (L + S) // 2
                oe = _dot(p_a, ve_ref[0:half, :]) + _dot(p_b, ve_ref[half:L + S, :])
            else:
                oe = _dot(p_a, ve_ref[0:L, :]) + _dot(p_b, ve_ref[pl.ds(L + start, nb), :])
            l = oe[:, HEAD_DIM:]
            if sink:
                l = l + jnp.exp2(sink_j - m)
            o = oe[:, :HEAD_DIM] / l
        o_ref[:, j * HEAD_DIM:(j + 1) * HEAD_DIM] = o.astype(o_ref.dtype)

    pending = scores(0)
    for j in range(1, KV_GROUP):
        nxt = scores(j)
        finish(j - 1, pending)
        pending = nxt
    finish(KV_GROUP - 1, pending)


def _attention(z1, cos, sin, q_gain, k_gain, sink_vec, *, mode, B, S, L, qcol, kcol, vcol):
    norm = q_gain is not None
    sink = sink_vec is not None
    lat0 = 0
    ctx0 = (B * S) // L
    if mode == 'ctx':
        tq, nt = L, 1
        q_map = lambda b, g, t: (ctx0 + b, qcol + g)
    else:
        tq = ATTN_ALL_ROWS if mode == 'all' else Q_BLOCK
        nt = S // tq
        q_map = lambda b, g, t: (lat0 + b * nt + t, qcol + g)
    qw = KV_GROUP * HEAD_DIM
    args, in_specs = [], []
    if sink:
        args.append(sink_vec)
        in_specs.append(pl.BlockSpec(memory_space=pltpu.SMEM))
    args += [z1, z1, z1]
    in_specs += [
        pl.BlockSpec((tq, qw), q_map),
        pl.BlockSpec((L, HEAD_DIM), lambda b, g, t: (ctx0 + b, kcol + g)),
        pl.BlockSpec((L, HEAD_DIM), lambda b, g, t: (ctx0 + b, vcol + g)),
    ]
    if mode != 'ctx':
        args += [z1, z1, cos, sin]
        in_specs += [
            pl.BlockSpec((S, HEAD_DIM), lambda b, g, t: (b, kcol + g)),
            pl.BlockSpec((S, HEAD_DIM), lambda b, g, t: (b, vcol + g)),
            pl.BlockSpec((S, HEAD_DIM), lambda b, g, t: (0, 0)),
            pl.BlockSpec((S, HEAD_DIM), lambda b, g, t: (0, 0)),
        ]
    if norm:
        args += [q_gain.reshape(1, HEAD_DIM), k_gain.reshape(1, HEAD_DIM)]
        in_specs += [pl.BlockSpec((1, HEAD_DIM), lambda b, g, t: (0, 0))] * 2
    scratch = [pltpu.VMEM((KV_GROUP * tq, HEAD_DIM), BF16)]
    assert not (mode == 'band' and norm)
    if mode == 'all':
        scratch.append(pltpu.VMEM((L + S, HEAD_DIM), BF16))
        scratch.append(pltpu.VMEM((L + S, 2 * HEAD_DIM), BF16))
    elif mode == 'band':
        scratch.append(pltpu.VMEM((S, HEAD_DIM), BF16))
        scratch.append(pltpu.VMEM((L + S, 2 * HEAD_DIM), BF16))
    elif norm:
        scratch.append(pltpu.VMEM((L, HEAD_DIM), BF16))
    n_rows = B * (L if mode == 'ctx' else S)
    if mode == 'ctx':
        o_map = lambda b, g, t: (b, g)
    else:
        o_map = lambda b, g, t: (b * nt + t, g)
    return pl.pallas_call(
        functools.partial(_attn_kernel, mode=mode, norm=norm, sink=sink, tq=tq, S=S, L=L),
        grid=(B, ATTN_KV_HEADS, nt),
        in_specs=in_specs,
        out_specs=pl.BlockSpec((tq, qw), o_map),
        out_shape=jax.ShapeDtypeStruct((n_rows, BRANCH_WIDTH), BF16),
        scratch_shapes=scratch,
        compiler_params=_cparams(("arbitrary", "arbitrary", "arbitrary")),
        name="attn_%s%s" % (mode, "_sink" if sink else ""),
    )(*args)


CONV_ROWS = 256
HALO = 16


def _gdn_conv_kernel(sfirst_ref, slast_ref, prev_ref, cur_ref, next_ref, w_ref, o_ref):
    r = pl.program_id(0)
    part = pl.program_id(1)
    n = cur_ref.shape[0]
    row = lax.broadcasted_iota(jnp.int32, (n, HEAD_DIM), 0)
    is_first = sfirst_ref[r] == 1
    is_last = slast_ref[r] == 1
    for h in range(GDN_HEADS):
        cols = slice(h * HEAD_DIM, (h + 1) * HEAD_DIM)
        x = cur_ref[:, cols].astype(F32)
        prev_row = jnp.where(is_first, 0.0, prev_ref[:, cols].astype(F32)[HALO - 1:HALO, :])
        next_row = jnp.where(is_last, 0.0, next_ref[:, cols].astype(F32)[0:1, :])
        xm = jnp.where(row == 0, prev_row, pltpu.roll(x, 1, 0))
        xp = jnp.where(row == n - 1, next_row, pltpu.roll(x, n - 1, 0))
        y = _silu(w_ref[0:1, cols] * xm + w_ref[1:2, cols] * x + w_ref[2:3, cols] * xp)
        rs = lax.rsqrt(jnp.sum(y * y, axis=-1, keepdims=True) + RMS_EPS)
        factor = jnp.where(part == 0, rs * (HEAD_DIM ** -0.5), jnp.where(part == 1, rs, 1.0))
        o_ref[:, cols] = y * factor


def _gdn_conv(z1, conv_w, seq_first, seq_last):
    M = z1.shape[0]
    nblk = M // CONV_ROWS
    W = GDN_HEADS * HEAD_DIM
    c0 = 3072 // W
    per = CONV_ROWS // HALO
    nh = M // HALO
    return pl.pallas_call(
        _gdn_conv_kernel,
        grid_spec=pltpu.PrefetchScalarGridSpec(
            num_scalar_prefetch=2, grid=(nblk, 3),
            in_specs=[
                pl.BlockSpec((HALO, W), lambda r, c, a, b: (jnp.maximum(r * per - 1, 0), c0 + c)),
                pl.BlockSpec((CONV_ROWS, W), lambda r, c, a, b: (r, c0 + c)),
                pl.BlockSpec((HALO, W), lambda r, c, a, b: (jnp.minimum((r + 1) * per, nh - 1), c0 + c)),
                pl.BlockSpec((3, W), lambda r, c, a, b: (0, c)),
            ],
            out_specs=pl.BlockSpec((CONV_ROWS, W), lambda r, c, a, b: (r, c))),
        out_shape=jax.ShapeDtypeStruct((M, 3 * W), F32),
        compiler_params=_cparams(("arbitrary", "arbitrary")),
        name="gdn_conv",
    )(seq_first, seq_last, z1, z1, z1, conv_w)


def _chunk_tri(n, upper):
    r = lax.broadcasted_iota(jnp.int32, (n, n), 0)
    c = lax.broadcasted_iota(jnp.int32, (n, n), 1)
    same = (r // CHUNK) == (c // CHUNK)
    tri = (c >= r) if upper else (c <= r)
    return jnp.where(same & tri, 1.0, 0.0).astype(BF16)


def _gdn_gates_kernel(zg_ref, a_ref, dtb_ref, o_ref):
    zg = zg_ref[...]
    n = zg.shape[0]
    lane = lax.broadcasted_iota(jnp.int32, zg.shape, 1)
    beta = _sigmoid(zg)
    t = zg + dtb_ref[...]
    softplus = jnp.maximum(t, 0.0) + jnp.log(1.0 + jnp.exp(-jnp.abs(t)))
    g = jnp.where((lane >= 16) & (lane < 32), a_ref[...] * softplus, 0.0)
    gc_f = _dot_exact_lhs(_chunk_tri(n, False), g)
    gc_b = _dot_exact_lhs(_chunk_tri(n, True), g)
    o_ref[...] = jnp.where(lane < 16, beta, jnp.where(lane < 24, gc_f, gc_b))


def _gdn_gates(zg, a_row, dtb_row):
    M = zg.shape[0]
    tm = 256
    return pl.pallas_call(
        _gdn_gates_kernel,
        grid=(M // tm,),
        in_specs=[pl.BlockSpec((tm, LANES), lambda i: (i, 0)),
                  pl.BlockSpec((1, LANES), lambda i: (0, 0)),
                  pl.BlockSpec((1, LANES), lambda i: (0, 0))],
        out_specs=pl.BlockSpec((tm, LANES), lambda i: (i, 0)),
        out_shape=jax.ShapeDtypeStruct((M, LANES), F32),
        compiler_params=_cparams(("arbitrary",)),
        name="gdn_gates",
    )(zg, a_row, dtb_row)


_INV_BASE = 8


def _split2(x):
    x1 = x.astype(BF16)
    return x1, (x - x1.astype(F32)).astype(BF16)


def _unit_tri_inverse(lms, r, c):
    C = lms[0].shape[0]
    eye = jnp.where(r == c, 1.0, 0.0)
    diag = (r // _INV_BASE) == (c // _INV_BASE)
    dms = [jnp.where(diag, lm, 0.0) for lm in lms]
    d16 = [dm.astype(BF16) for dm in dms]
    p16 = [_dot(d, d).astype(BF16) for d in d16]
    xs = [eye - dm for dm in dms]
    rrs = [_dot(jnp.concatenate([p, x.astype(BF16)], axis=0), p) for p, x in zip(p16, xs)]
    xs = [x + rr[C:] for x, rr in zip(xs, rrs)]
    ts = [x + _dot(x.astype(BF16), rr[:C].astype(BF16)) for x, rr in zip(xs, rrs)]
    s = _INV_BASE
    while s < C:
        join = ((r // (2 * s)) == (c // (2 * s))) & ((r // s) != (c // s))
        e16 = [jnp.where(join, lm, 0.0).astype(BF16) for lm in lms]
        t16 = [t.astype(BF16) for t in ts]
        ys = [_dot(e, t).astype(BF16) for e, t in zip(e16, t16)]
        ts = [t - _dot(tb, y) for t, tb, y in zip(ts, t16, ys)]
        s *= 2
    return ts


GDN_PREP_CHUNKS = 8


def _gdn_prep_kernel(q_ref, k_ref, v_ref, g_ref, rf_ref, rb_ref, wq_ref, u_ref, qkkd_ref, gl_ref):
    C = CHUNK
    h = pl.program_id(1)
    r = lax.broadcasted_iota(jnp.int32, (C, C), 0)
    c = lax.broadcasted_iota(jnp.int32, (C, C), 1)
    lane = lax.broadcasted_iota(jnp.int32, (C, LANES), 1)
    row_refs = (rf_ref, rb_ref)
    qs, ks, vs, kks, qks, gts = [], [], [], [], [], []
    for n in range(GDN_PREP_CHUNKS):
        rows = slice(n * C, (n + 1) * C)
        qs.append(q_ref[rows, :])
        ks.append(k_ref[rows, :])
        vs.append(v_ref[rows, :])
        gts.append(g_ref[rows, :])
        kb16 = ks[n].astype(BF16)
        kks.append(_dot_nt(kb16, kb16))
        qks.append(_dot_nt(qs[n].astype(BF16), kb16))

    def column(tile, ch):
        return jnp.sum(jnp.where(lane == ch, tile, 0.0), axis=-1, keepdims=True)

    units = [(n, d) for n in range(GDN_PREP_CHUNKS) for d in range(2)]
    lms, rhs1, rhs2 = [], [], []
    for n, d in units:
        beta = column(gts[n], d * GDN_HEADS + h)
        gc = column(gts[n], 16 + d * GDN_HEADS + h)
        grow = row_refs[d][0, n][0:1, :]
        incl = (c <= r) if d == 0 else (c >= r)
        strict = (c < r) if d == 0 else (c > r)
        last = C - 1 if d == 0 else 0
        decay = jnp.where(incl, jnp.exp(jnp.where(incl, gc - grow, 0.0)), 0.0)
        lms.append(jnp.where(strict, beta * kks[n] * decay, 0.0))
        eg = jnp.exp(gc)
        g_last = gc[last:last + 1, :]
        rhs = jnp.concatenate([ks[n] * (beta * eg), vs[n] * beta], axis=-1)
        r1, r2 = _split2(rhs)
        rhs1.append(r1)
        rhs2.append(r2)
        wq_ref[d, (2 * n + 1) * C:(2 * n + 2) * C, :] = (qs[n] * eg).astype(BF16)
        kd = ks[n] * jnp.exp(g_last - gc)
        base = n * (C + HEAD_DIM)
        qkkd_ref[d, 0, base:base + C, :] = (qks[n] * decay).astype(BF16)
        qkkd_ref[d, 0, base + C:base + C + HEAD_DIM, :] = jnp.transpose(kd).astype(BF16)
        gl_ref[d, 0, n * 8:(n + 1) * 8, :] = jnp.broadcast_to(jnp.exp(g_last), (8, HEAD_DIM))
    t16 = [t.astype(BF16) for t in _unit_tri_inverse(lms, r, c)]
    wa = [_dot(t, r1) for t, r1 in zip(t16, rhs1)]
    wb = [_dot(t, r2) for t, r2 in zip(t16, rhs2)]
    for (n, d), a, b in zip(units, wa, wb):
        wu = a + b
        wq_ref[d, 2 * n * C:(2 * n + 1) * C, :] = wu[:, :HEAD_DIM].astype(BF16)
        u_ref[d, n * C:(n + 1) * C, :] = wu[:, HEAD_DIM:]


def _gdn_prep(qkv, gates, rowb):
    M = qkv.shape[0]
    H = GDN_HEADS
    N = GDN_PREP_CHUNKS
    C = CHUNK
    nch = M // (C * N)
    rb_spec = lambda ch0: pl.BlockSpec((1, N, 8, C), lambda c, h: (ch0 + h, c, 0, 0))
    return pl.pallas_call(
        _gdn_prep_kernel,
        grid=(nch, H),
        in_specs=[
            pl.BlockSpec((N * C, HEAD_DIM), lambda c, h: (c, h)),
            pl.BlockSpec((N * C, HEAD_DIM), lambda c, h: (c, H + h)),
            pl.BlockSpec((N * C, HEAD_DIM), lambda c, h: (c, 2 * H + h)),
            pl.BlockSpec((N * C, LANES), lambda c, h: (c, 0)),
            rb_spec(0), rb_spec(H),
        ],
        out_specs=[
            pl.BlockSpec((2, N * 2 * C, HEAD_DIM), lambda c, h: (0, c, h)),
            pl.BlockSpec((2, N * C, HEAD_DIM), lambda c, h: (0, c, h)),
            pl.BlockSpec((2, 1, N * (C + HEAD_DIM), C), lambda c, h: (0, h, c, 0)),
            pl.BlockSpec((2, 1, N * 8, HEAD_DIM), lambda c, h: (0, h, c, 0)),
        ],
        name="gdn_prep",
        out_shape=[
            jax.ShapeDtypeStruct((2, 2 * M, H * HEAD_DIM), BF16),
            jax.ShapeDtypeStruct((2, M, H * HEAD_DIM), F32),
            jax.ShapeDtypeStruct((2, H, (M // C) * (C + HEAD_DIM), C), BF16),
            jax.ShapeDtypeStruct((2, H, (M // C) * 8, HEAD_DIM), F32),
        ],
        compiler_params=_cparams(("arbitrary", "arbitrary")),
    )(qkv, qkv, qkv, gates, rowb, rowb)


def _gdn_state_kernel(wqf_ref, uf_ref, qkf_ref, glf_ref, wqb_ref, ub_ref, qkb_ref, glb_ref,
                      of_ref, ob_ref, s_ref):
    C = CHUNK
    H = GDN_HEADS

    @pl.when(pl.program_id(1) == 0)
    def _():
        s_ref[...] = jnp.zeros(s_ref.shape, F32)

    dirs = ((wqf_ref, uf_ref, qkf_ref, glf_ref, of_ref), (wqb_ref, ub_ref, qkb_ref, glb_ref, ob_ref))
    units = [(d, h) for d in range(2) for h in range(H)]
    states = [s_ref[d * H + h] for d, h in units]
    for step in range(SCAN_CHUNKS):
        subs = [step if d == 0 else SCAN_CHUNKS - 1 - step for d, _ in units]
        m1s = []
        for (d, h), sub, st in zip(units, subs, states):
            cols = slice(h * HEAD_DIM, (h + 1) * HEAD_DIM)
            m1s.append(_dot(dirs[d][0][0, sub * 2 * C:(sub + 1) * 2 * C, cols], st.astype(BF16)))
        m2s = []
        for (d, h), sub, m1 in zip(units, subs, m1s):
            cols = slice(h * HEAD_DIM, (h + 1) * HEAD_DIM)
            v_new = dirs[d][1][0, sub * C:(sub + 1) * C, cols] - m1[0:C]
            m2s.append(_dot(dirs[d][2][0, h, sub * (C + HEAD_DIM):(sub + 1) * (C + HEAD_DIM), :],
                            v_new.astype(BF16)))
        new_states = []
        for (d, h), sub, st, m1, m2 in zip(units, subs, states, m1s, m2s):
            cols = slice(h * HEAD_DIM, (h + 1) * HEAD_DIM)
            dirs[d][4][sub * C:(sub + 1) * C, cols] = (m1[C:2 * C] + m2[0:C]).astype(BF16)
            new_states.append(st * dirs[d][3][0, h, sub * 8:sub * 8 + 1, :] + m2[C:])
        states = new_states
    for (d, h), st in zip(units, states):
        s_ref[d * H + h] = st


SCAN_CHUNKS = 2


def _seq_chunk_maps(B, S, L):
    C = CHUNK * SCAN_CHUNKS
    lc, sc = L // C, S // C
    ctx0 = (B * S) // C

    def fwd(b, c):
        return jnp.where(c < lc, ctx0 + b * lc + c, b * sc + (c - lc))

    def bwd(b, c):
        return jnp.where(c < lc, ctx0 + b * lc + (lc - 1 - c), b * sc + (sc - 1 - (c - lc)))

    return fwd, bwd, lc + sc


def _gdn_state(wq, u, qkkd, gl, *, B, S, L):
    C = CHUNK
    H = GDN_HEADS
    M = u.shape[1]
    fwd, bwd, nc = _seq_chunk_maps(B, S, L)

    N = SCAN_CHUNKS

    def specs(d, pos):
        return [
            pl.BlockSpec((1, N * 2 * C, H * HEAD_DIM), lambda b, c: (d, pos(b, c), 0)),
            pl.BlockSpec((1, N * C, H * HEAD_DIM), lambda b, c: (d, pos(b, c), 0)),
            pl.BlockSpec((1, H, N * (C + HEAD_DIM), C), lambda b, c: (d, 0, pos(b, c), 0)),
            pl.BlockSpec((1, H, N * 8, HEAD_DIM), lambda b, c: (d, 0, pos(b, c), 0)),
        ]

    return pl.pallas_call(
        _gdn_state_kernel,
        grid=(B, nc),
        in_specs=specs(0, fwd) + specs(1, bwd),
        out_specs=[pl.BlockSpec((N * C, H * HEAD_DIM), lambda b, c: (fwd(b, c), 0)),
                   pl.BlockSpec((N * C, H * HEAD_DIM), lambda b, c: (bwd(b, c), 0))],
        out_shape=[jax.ShapeDtypeStruct((M, H * HEAD_DIM), BF16)] * 2,
        scratch_shapes=[pltpu.VMEM((2 * H, HEAD_DIM, HEAD_DIM), F32)],
        compiler_params=_cparams(("arbitrary", "arbitrary")),
        name="gdn_state",
    )(wq, u, qkkd, gl, wq, u, qkkd, gl)


def _gla_kernel(qf_ref, kf_ref, vf_ref, zf_ref, qb_ref, kb_ref, vb_ref, zb_ref, wgk_ref, bgk_ref,
                of_ref, ob_ref, s_ref):
    C = CHUNK
    r = lax.broadcasted_iota(jnp.int32, (C, C), 0)
    c = lax.broadcasted_iota(jnp.int32, (C, C), 1)

    @pl.when(pl.program_id(1) == 0)
    def _():
        s_ref[...] = jnp.zeros(s_ref.shape, F32)

    dirs = ((qf_ref, kf_ref, vf_ref, zf_ref, of_ref), (qb_ref, kb_ref, vb_ref, zb_ref, ob_ref))
    incls = [(c <= r), (c >= r)]
    nrows = SCAN_CHUNKS * C
    bcums = []
    for d in range(2):
        pre = _dot_hi(dirs[d][3][...], wgk_ref[d]) + bgk_ref[d]
        gk = (jnp.minimum(pre, 0.0) - jnp.log(1.0 + jnp.exp(-jnp.abs(pre)))) * (1.0 / GLA_GATE_NORM)
        bcums.append(_dot_exact_lhs(_chunk_tri(nrows, d == 1), gk))
    units = [(d, h) for d in range(2) for h in range(GLA_HEADS)]
    states = [s_ref[d * GLA_HEADS + h] for d, h in units]
    for step in range(SCAN_CHUNKS):
        scores, inter, upd, dls = [], [], [], []
        for (d, h), st in zip(units, states):
            sub = step if d == 0 else SCAN_CHUNKS - 1 - step
            rows = slice(sub * C, (sub + 1) * C)
            kc = slice(h * GLA_DK, (h + 1) * GLA_DK)
            vc = slice(h * GLA_DV, (h + 1) * GLA_DV)
            mid = C // 2 - 1 if d == 1 else C // 2
            last = 0 if d == 1 else C - 1
            q = dirs[d][0][rows, kc].astype(F32) * (GLA_DK ** -0.5)
            k = dirs[d][1][rows, kc].astype(F32)
            b = bcums[d][sub * C:(sub + 1) * C, kc]
            b_mid = b[mid:mid + 1, :]
            b_last = b[last:last + 1, :]
            scores.append(_dot_nt((q * jnp.exp(b - b_mid)).astype(BF16), (k * jnp.exp(b_mid - b)).astype(BF16)))
            inter.append(_dot((q * jnp.exp(b)).astype(BF16), st.astype(BF16)))
            upd.append(_dot_tn((k * jnp.exp(b_last - b)).astype(BF16), dirs[d][2][rows, vc]))
            dls.append(jnp.transpose(jnp.broadcast_to(jnp.exp(b_last), (GLA_DK, GLA_DK))))
        new_states = []
        for (d, h), st, sc, oi, up, dl in zip(units, states, scores, inter, upd, dls):
            sub = step if d == 0 else SCAN_CHUNKS - 1 - step
            rows = slice(sub * C, (sub + 1) * C)
            vc = slice(h * GLA_DV, (h + 1) * GLA_DV)
            p = jnp.where(incls[d], sc, 0.0).astype(BF16)
            dirs[d][4][rows, vc] = (_dot(p, dirs[d][2][rows, vc]) + oi).astype(BF16)
            new_states.append(st * jnp.concatenate([dl, dl], axis=-1) + up)
        states = new_states
    for (d, h), st in zip(units, states):
        s_ref[d * GLA_HEADS + h] = st


def _gla(z2, zg, wgk_pad, bgk, *, B, S, L):
    C = CHUNK * SCAN_CHUNKS
    M = z2.shape[0]
    fwd, bwd, nc = _seq_chunk_maps(B, S, L)
    kw = GLA_HEADS * GLA_DK
    vw = GLA_HEADS * GLA_DV

    def specs(pos):
        return [
            pl.BlockSpec((C, kw), lambda b, c: (pos(b, c), 0)),
            pl.BlockSpec((C, kw), lambda b, c: (pos(b, c), 1)),
            pl.BlockSpec((C, vw), lambda b, c: (pos(b, c), 1)),
            pl.BlockSpec((C, LANES), lambda b, c: (pos(b, c), 0)),
        ]

    return pl.pallas_call(
        _gla_kernel,
        grid=(B, nc),
        in_specs=specs(fwd) + specs(bwd) + [
            pl.BlockSpec((2, LANES, kw), lambda b, c: (0, 0, 0)),
            pl.BlockSpec((2, 1, kw), lambda b, c: (0, 0, 0)),
        ],
        out_specs=[pl.BlockSpec((C, vw), lambda b, c: (fwd(b, c), 0)),
                   pl.BlockSpec((C, vw), lambda b, c: (bwd(b, c), 0))],
        out_shape=[jax.ShapeDtypeStruct((M, vw), BF16)] * 2,
        scratch_shapes=[pltpu.VMEM((2 * GLA_HEADS, GLA_DK, GLA_DV), F32)],
        compiler_params=_cparams(("arbitrary", "arbitrary")),
        name="gla",
    )(z2, z2, z2, zg, z2, z2, z2, zg, wgk_pad, bgk)


def _headnorm_kernel(of_ref, ob_ref, z_ref, g_ref, o_ref, *, hd):
    n = of_ref.shape[1] // hd
    for h in range(n):
        cols = slice(h * hd, (h + 1) * hd)
        o = of_ref[:, cols].astype(F32) + ob_ref[:, cols].astype(F32)
        y = o * lax.rsqrt(jnp.mean(o * o, axis=-1, keepdims=True) + RMS_EPS) * g_ref[...]
        o_ref[:, cols] = (y * _silu(z_ref[:, cols].astype(F32))).astype(o_ref.dtype)


def _headnorm(of, ob, z, zcol, gain, *, hd, tm=256):
    M, W = of.shape
    return pl.pallas_call(
        functools.partial(_headnorm_kernel, hd=hd),
        grid=(M // tm,),
        in_specs=[pl.BlockSpec((tm, W), lambda i: (i, 0)),
                  pl.BlockSpec((tm, W), lambda i: (i, 0)),
                  pl.BlockSpec((tm, W), lambda i: (i, zcol)),
                  pl.BlockSpec((1, hd), lambda i: (0, 0))],
        out_specs=pl.BlockSpec((tm, W), lambda i: (i, 0)),
        out_shape=jax.ShapeDtypeStruct((M, W), BF16),
        compiler_params=_cparams(("arbitrary",)),
        name="headnorm",
    )(of, ob, z, gain.reshape(1, hd))


def _final_kernel(rowid_ref, *refs, with_y, norm):
    del rowid_ref
    refs = list(refs)
    x = refs.pop(0)[...]
    if with_y:
        y0_ref, y1_ref, gt_ref = refs[:3]
        refs = refs[3:]
        x = x + gt_ref[0] * (y0_ref[...].astype(F32) + y1_ref[...].astype(F32))
    if norm:
        g_ref = refs.pop(0)
        x = x * lax.rsqrt(jnp.mean(x * x, axis=-1, keepdims=True) + RMS_EPS) * g_ref[...]
    refs[0][...] = x


def _final(X, ys, mod3, rowid, layer, k_gate, g_final, *, tm, grid_m):
    D = X.shape[1]
    row_spec = pl.BlockSpec((tm, D), lambda i, r: (i, 0))
    args, in_specs = [X], [row_spec]
    if ys is not None:
        args += [ys[0], ys[1], mod3]
        in_specs += [row_spec, row_spec,
                     pl.BlockSpec((1, 1, D), lambda i, r: (layer * 16 + r[i], 0, k_gate))]
    if g_final is not None:
        args.append(g_final.reshape(1, D))
        in_specs.append(pl.BlockSpec((1, D), lambda i, r: (0, 0)))
    return pl.pallas_call(
        functools.partial(_final_kernel, with_y=ys is not None, norm=g_final is not None),
        grid_spec=pltpu.PrefetchScalarGridSpec(
            num_scalar_prefetch=1, grid=(grid_m,), in_specs=in_specs, out_specs=row_spec),
        out_shape=jax.ShapeDtypeStruct((grid_m * tm, D), F32),
        compiler_params=_cparams(("arbitrary",)),
        name="final",
    )(rowid, *args)


def _rope_tables(S):
    rows = S // GRID_W
    row = jnp.repeat(jnp.arange(rows, dtype=F32), GRID_W)
    col = jnp.tile(jnp.arange(GRID_W, dtype=F32), rows)
    inv_freq = ROPE_THETA ** (-jnp.arange(ROPE_FREQS, dtype=F32) / ROPE_FREQS)
    ar = row[:, None] * inv_freq
    ac = col[:, None] * inv_freq
    cos = jnp.concatenate([jnp.cos(ar), jnp.cos(ar), jnp.cos(ac), jnp.cos(ac)], axis=-1)
    sin = jnp.concatenate([-jnp.sin(ar), jnp.sin(ar), -jnp.sin(ac), jnp.sin(ac)], axis=-1)
    return cos, sin


def _stream_rowid(B, S, L, tm, grid_m):
    i = jnp.arange(grid_m, dtype=jnp.int32) * tm
    return jnp.where(i < B * S, i // S, B).astype(jnp.int32)


def _moe_plan(logits, T, tm, n_tiles):
    top_val, top_idx = lax.top_k(logits, TOP_K)
    top_w = jax.nn.softmax(top_val, axis=-1)
    e_flat = top_idx.reshape(-1).astype(jnp.int32)
    w_flat = top_w.reshape(-1)
    n_assign = T * TOP_K
    order = jnp.argsort(e_flat, stable=True).astype(jnp.int32)
    inv = jnp.argsort(order).astype(jnp.int32)
    experts = jnp.arange(N_EXPERTS, dtype=jnp.int32)
    sizes = jnp.sum((e_flat[:, None] == experts[None, :]).astype(jnp.int32), axis=0)
    start = jnp.cumsum(sizes) - sizes
    padded = ((sizes + tm - 1) // tm) * tm
    pend = jnp.cumsum(padded)
    pstart = pend - padded
    dest = pstart[e_flat] + inv - start[e_flat]
    tile0 = jnp.arange(n_tiles, dtype=jnp.int32) * tm
    tile_e = jnp.minimum(jnp.searchsorted(pend, tile0, side='right'), N_EXPERTS - 1).astype(jnp.int32)
    valid = (tile0 < pend[-1]).astype(jnp.int32)
    row = jnp.arange(n_tiles * tm, dtype=jnp.int32)
    row_e = jnp.repeat(tile_e, tm)
    row_rank = row - pstart[row_e]
    row_ok = (row_rank < sizes[row_e]) & (jnp.repeat(valid, tm) == 1)
    row_a = order[jnp.clip(start[row_e] + row_rank, 0, n_assign - 1)]
    row_token = jnp.where(row_ok, row_a // TOP_K, 0).astype(jnp.int32)
    row_w = jnp.where(row_ok, w_flat[row_a], 0.0)
    last_e = tile_e[jnp.maximum(pend[-1] // tm - 1, 0)]
    tile_e = jnp.where(valid == 1, tile_e, last_e)
    first = jnp.concatenate([jnp.ones((1,), jnp.int32), (tile_e[1:] != tile_e[:-1]).astype(jnp.int32)])
    return row_token, row_w, dest.reshape(T, TOP_K), tile_e, first, valid


def kernel(x, c, ctx, c_ctx, w_ada, b_ada, g_norm_mix, g_norm_ffn, w_in, a_q_gain, a_k_gain, b_sink,
           gdn_conv, gdn_a_log, gdn_dt_bias, gdn_norm_gain, gla_w_gk, gla_b_gk, gla_norm_gain,
           w_branch, w_merge, w_out, w1_dense, w3_dense, w2_dense, w_router, w1_moe, w3_moe, w2_moe,
           g_final):
    B, S, D = x.shape
    L = ctx.shape[1]
    depth = w_in.shape[0]
    n_lat = B * S
    M = n_lat + B * L
    TM = 512
    TM_BIG = 1024
    assert M % (CHUNK * GDN_PREP_CHUNKS) == 0 and L % (CHUNK * SCAN_CHUNKS) == 0
    assert D == D_MODEL and S % CONV_ROWS == 0 and L % CONV_ROWS == 0 and n_lat % TM == 0
    assert (B * L) % TM == 0 and S % TM == 0 and S >= 3 * Q_BLOCK and n_lat % L == 0 and B <= 15

    cond = jnp.zeros((16, D), F32).at[:B].set(c).at[B].set(c_ctx)
    cond2 = jnp.concatenate([cond] * depth, axis=0)
    (mod,) = _panel_call(
        _ep_ada, name="ada", grid_n=(6 * D) // 512, grid_m=depth,
        lhs=[(cond2, (16, D), lambda j, i, e, f, v, r: (i, 0))],
        weights=[(w_ada, (1, D, 512), lambda j, i, e, f, v, r: (e[i], 0, j))],
        extras=[(b_ada.reshape(depth, 1, 6 * D), (1, 1, 512), lambda j, i, e, f, v, r: (e[i], 0, j))],
        outs=[(jax.ShapeDtypeStruct((16 * depth, 6 * D), F32), (16, 512), lambda j, i, e, f, v, r: (i, j))],
        eid=jnp.arange(depth, dtype=jnp.int32), first=jnp.ones((depth,), jnp.int32))
    mod3 = mod.reshape(16 * depth, 1, 6 * D)

    cos, sin = _rope_tables(S)
    X = jnp.concatenate([x.reshape(n_lat, D), ctx.reshape(B * L, D)], axis=0)
    gm_all = M // TM
    gm_lat = n_lat // TM
    rowid = _stream_rowid(B, S, L, TM, gm_all)
    rowid256 = _stream_rowid(B, S, L, 256, M // 256)

    blk = jnp.arange(M // CONV_ROWS, dtype=jnp.int32) * CONV_ROWS
    seq_len = jnp.where(blk < n_lat, S, L)
    seq_off = jnp.where(blk < n_lat, blk, blk - n_lat)
    seq_first = (seq_off % seq_len == 0).astype(jnp.int32)
    seq_last = ((seq_off + CONV_ROWS) % seq_len == 0).astype(jnp.int32)

    out = None
    for layer in range(depth):
        last = layer == depth - 1
        gm = gm_lat if last else gm_all
        h = _modulate(X, g_norm_mix, mod3, rowid256, layer, 0, 1, tm=256, grid_m=M // 256)
        tm_z = TM_BIG if M % TM_BIG == 0 else TM
        z1 = _matmul_stream(h, w_in, layer, 0, Z1_WIDTH, name="proj_z1", tm=tm_z, tn=1024, out_dtype=BF16)
        w_d = w_in[layer][:, COL_D_Q:COL_D_GATE]
        z2 = _matmul_stream(h, w_d, layer, 0, Z2_WIDTH, name="proj_z2", tm=tm_z, tn=1024, out_dtype=BF16)
        w_g = jnp.concatenate([w_in[layer][:, COL_C_BETA:COL_D_Q], w_in[layer][:, COL_D_GATE:],
                               jnp.zeros((D, LANES - 64), F32)], axis=1)
        zg = _matmul_stream(h, w_g, layer, 0, LANES, name="proj_gates", tm=TM, tn=LANES, out_dtype=F32)

        oa = _attention(z1, cos, sin, a_q_gain[layer], a_k_gain[layer], None, mode='all',
                        B=B, S=S, L=L, qcol=0, kcol=8, vcol=10)
        ob = _attention(z1, cos, sin, None, None, b_sink[layer], mode='band',
                        B=B, S=S, L=L, qcol=3, kcol=20, vcol=22)
        if not last:
            oa_c = _attention(z1, cos, sin, a_q_gain[layer], a_k_gain[layer], None, mode='ctx',
                              B=B, S=S, L=L, qcol=0, kcol=8, vcol=10)
            ob_c = _attention(z1, cos, sin, None, None, b_sink[layer], mode='ctx',
                              B=B, S=S, L=L, qcol=3, kcol=20, vcol=22)
            oa = jnp.concatenate([oa, oa_c], axis=0)
            ob = jnp.concatenate([ob, ob_c], axis=0)

        qkv = _gdn_conv(z1, gdn_conv[layer], seq_first, seq_last)
        a_row = jnp.zeros((1, LANES), F32).at[0, 16:32].set(-jnp.exp(gdn_a_log[layer]).reshape(-1))
        dtb_row = jnp.zeros((1, LANES), F32).at[0, 16:32].set(gdn_dt_bias[layer].reshape(-1))
        gates = _gdn_gates(zg, a_row, dtb_row)
        gt = gates[:, 16:32].T
        rowb = jnp.broadcast_to(gt.reshape(16, M // CHUNK, 1, CHUNK), (16, M // CHUNK, 8, CHUNK))
        wq, u, qkkd, gl = _gdn_prep(qkv, gates, rowb)
        oc_f, oc_b = _gdn_state(wq, u, qkkd, gl, B=B, S=S, L=L)
        oc = _headnorm(oc_f, oc_b, z1, 6144 // BRANCH_WIDTH, gdn_norm_gain[layer], hd=HEAD_DIM)

        wgk_pad = jnp.zeros((2, LANES, GLA_HEADS * GLA_DK), F32)
        wgk_pad = wgk_pad.at[0, 32:48].set(gla_w_gk[layer, 0]).at[1, 48:64].set(gla_w_gk[layer, 1])
        od_f, od_b = _gla(z2, zg, wgk_pad, gla_b_gk[layer].reshape(2, 1, -1), B=B, S=S, L=L)
        od = _headnorm(od_f, od_b, z2, 2048 // BRANCH_WIDTH, gla_norm_gain[layer], hd=GLA_DV)

        tn_m = 256
        branches = [oa, ob, oc, od]
        lhs = [(h, (TM, D), lambda j, i, e, f, v, r: (i, 0))]
        lhs += [(o, (TM, BRANCH_WIDTH), lambda j, i, e, f, v, r: (i, 0)) for o in branches]
        wts = [(w_merge, (1, D, tn_m), lambda j, i, e, f, v, r, n=n: (layer, 0, n * (D // tn_m) + j))
               for n in range(N_BRANCH)]
        wts += [(w_branch, (1, 1, BRANCH_WIDTH, tn_m), lambda j, i, e, f, v, r, n=n: (layer, n, 0, j))
                for n in range(N_BRANCH)]
        (merged,) = _panel_call(
            _ep_merge, name="merge", grid_n=D // tn_m, grid_m=gm, lhs=lhs, weights=wts, extras=[],
            outs=[(jax.ShapeDtypeStruct((gm * TM, D), BF16), (TM, tn_m), lambda j, i, e, f, v, r: (i, j))])

        def resid(name, xin, w, wmap, kdim, k_gate, tn):
            (res,) = _panel_call(
                _ep_resid, name=name, grid_n=D // tn, grid_m=gm,
                lhs=[(xin, (TM, kdim), lambda j, i, e, f, v, r: (i, 0))],
                weights=[(w, (1, kdim, tn), wmap)],
                extras=[(X, (TM, tn), lambda j, i, e, f, v, r: (i, j)),
                        (mod3, (1, 1, tn), lambda j, i, e, f, v, r: (layer * 16 + r[i], 0, k_gate * (D // tn) + j))],
                outs=[(jax.ShapeDtypeStruct((gm * TM, D), F32), (TM, tn), lambda j, i, e, f, v, r: (i, j))],
                rowid=rowid[:gm])
            return res

        X = resid("out_proj", merged, w_out,lambda j, i, e, f, v, r: (layer, 0, j), D, 2, 512)

        if layer % 2 == 0:
            li = layer // 2
            h2 = _modulate(X, g_norm_ffn, mod3, rowid256, layer, 3, 4, tm=256, grid_m=(gm * TM) // 256)
            d_ff = w1_dense.shape[2]
            tm_u = TM_BIG if (gm * TM) % TM_BIG == 0 else TM
            (uu,) = _panel_call(
                _ep_swiglu, name="dense_up", grid_n=d_ff // 512, grid_m=(gm * TM) // tm_u,
                lhs=[(h2, (tm_u, D), lambda j, i, e, f, v, r: (i, 0))],
                weights=[(w1_dense, (1, D, 512), lambda j, i, e, f, v, r: (li, 0, j)),
                         (w3_dense, (1, D, 512), lambda j, i, e, f, v, r: (li, 0, j))],
                extras=[],
                outs=[(jax.ShapeDtypeStruct((gm * TM, d_ff), BF16), (tm_u, 512), lambda j, i, e, f, v, r: (i, j))])
            X = resid("dense_down", uu, w2_dense, lambda j, i, e, f, v, r: (li, 0, j), d_ff, 5, 512)
            if last:
                out = _final(X, None, mod3, rowid256, layer, 5, g_final, tm=256, grid_m=n_lat // 256)
        else:
            li = layer // 2
            T = gm * TM
            wr_pad = jnp.zeros((D, LANES), F32).at[:, :N_EXPERTS].set(w_router[li])
            h2, logits = _modulate(X, g_norm_ffn, mod3, rowid256, layer, 3, 4, tm=256,
                                   grid_m=T // 256, w_router=wr_pad)
            tm_e = 512
            n_tiles = (T * TOP_K) // tm_e + N_EXPERTS
            row_token, row_w, dest, tile_e, first, valid = _moe_plan(logits[:, :N_EXPERTS], T, tm_e, n_tiles)
            xg = jnp.take(h2, row_token, axis=0)
            d_ffe = w1_moe.shape[3]
            (uu,) = _panel_call(
                _ep_swiglu, name="moe_up", grid_n=d_ffe // 1024, grid_m=n_tiles,
                lhs=[(xg, (tm_e, D), lambda j, i, e, f, v, r: (i, 0))],
                weights=[(w1_moe, (1, 1, D, 1024), lambda j, i, e, f, v, r: (li, e[i], 0, j)),
                         (w3_moe, (1, 1, D, 1024), lambda j, i, e, f, v, r: (li, e[i], 0, j))],
                extras=[],
                outs=[(jax.ShapeDtypeStruct((n_tiles * tm_e, d_ffe), BF16), (tm_e, 1024),
                       lambda j, i, e, f, v, r: (i, j))],
                eid=tile_e, first=first, valid=valid)
            tm_d = 256
            tn_d = 512
            rep = tm_e // tm_d
            first_d = jnp.repeat(first, rep) * (jnp.arange(n_tiles * rep, dtype=jnp.int32) % rep == 0)
            (yo,) = _panel_call(
                _ep_rowscale, name="moe_down", grid_n=D // tn_d, grid_m=n_tiles * rep,
                lhs=[(uu, (tm_d, d_ffe), lambda j, i, e, f, v, r: (i, 0))],
                weights=[(w2_moe, (1, 1, d_ffe, tn_d), lambda j, i, e, f, v, r: (li, e[i], 0, j))],
                extras=[(row_w.reshape(-1, 1), (tm_d, 1), lambda j, i, e, f, v, r: (i, 0))],
                outs=[(jax.ShapeDtypeStruct((n_tiles * tm_e, D), BF16), (tm_d, tn_d),
                       lambda j, i, e, f, v, r: (i, j))],
                eid=jnp.repeat(tile_e, rep), first=first_d.astype(jnp.int32), valid=jnp.repeat(valid, rep))
            ys = (jnp.take(yo, dest[:, 0], axis=0), jnp.take(yo, dest[:, 1], axis=0))
            res = _final(X, ys, mod3, rowid256, layer, 5, g_final if last else None, tm=256,
                         grid_m=T // 256)
            if last:
                out = res
            else:
                X = res
    return out.reshape(B, S, D)
```

```python
import functools

import jax
import jax.numpy as jnp
from jax import lax
from jax.experimental import pallas as pl
from jax.experimental.pallas import tpu as pltpu

F32 = jnp.float32
BF16 = jnp.bfloat16

D_MODEL = 2048
HEAD_DIM = 128
GRID_W = 64
ROPE_THETA = 10000.0
ROPE_FREQS = HEAD_DIM // 4
RMS_EPS = 1e-6
BRANCH_WIDTH = D_MODEL // 2
KV_GROUP = 4
ATTN_KV_HEADS = 2
WINDOW = 128
Q_BLOCK = 128
GDN_HEADS = 8
GLA_HEADS = 4
GLA_DK = 128
GLA_DV = 256
GLA_RANK = 16
GLA_GATE_NORM = 16.0
CHUNK = 64
N_EXPERTS = 8
TOP_K = 2
N_BRANCH = 4
NEG_BIG = -1e30
LOG2E = 1.4426950408889634

COL_C_BETA = 7168
COL_D_Q = 7200
COL_D_GATE = 10272
PROJ_WIDTH = 10304
Z1_WIDTH = 7168
Z2_WIDTH = 3072

VMEM_LIMIT = 56 * 1024 * 1024
LANES = 128


def _cparams(sem):
    return pltpu.CompilerParams(dimension_semantics=sem, vmem_limit_bytes=VMEM_LIMIT)


def _silu(x):
    return x * (1.0 / (1.0 + jnp.exp(-x)))


def _sigmoid(x):
    return 1.0 / (1.0 + jnp.exp(-x))


def _split3(x):
    x1 = x.astype(BF16)
    r1 = x - x1.astype(F32)
    x2 = r1.astype(BF16)
    x3 = (r1 - x2.astype(F32)).astype(BF16)
    return x1, x2, x3


def _dot(a, b):
    return jnp.dot(a, b, preferred_element_type=F32)


def _dot_nt(a, b):
    return lax.dot_general(a, b, (((1,), (1,)), ((), ())), preferred_element_type=F32)


def _dot_tn(a, b):
    return lax.dot_general(a, b, (((0,), (0,)), ((), ())), preferred_element_type=F32)


def _dot_exact_lhs(tri_bf16, x):
    x1, x2, x3 = _split3(x)
    return _dot(tri_bf16, x1) + _dot(tri_bf16, x2) + _dot(tri_bf16, x3)


def _dot_hi(a, b):
    a1 = a.astype(BF16)
    a2 = (a - a1.astype(F32)).astype(BF16)
    b1 = b.astype(BF16)
    b2 = (b - b1.astype(F32)).astype(BF16)
    return _dot(a1, b1) + _dot(a1, b2) + _dot(a2, b1)


_CAST_ROWS = 256


def _panel_kernel(eid_ref, first_ref, valid_ref, rowid_ref, *refs, n_lhs, n_w, n_ex, n_out,
                  w_rows, epilogue):
    del eid_ref, rowid_ref
    lhs = refs[:n_lhs]
    ws = refs[n_lhs:n_lhs + n_w]
    exs = refs[n_lhs + n_w:n_lhs + n_w + n_ex]
    outs = refs[n_lhs + n_w + n_ex:n_lhs + n_w + n_ex + n_out]
    wbs = refs[n_lhs + n_w + n_ex + n_out:]
    i = pl.program_id(1)

    @pl.when(first_ref[i] == 1)
    def _():
        for w_ref, wb_ref, rows in zip(ws, wbs, w_rows):
            lead = (0,) * (len(w_ref.shape) - 2)

            def cast(r, carry, w_ref=w_ref, wb_ref=wb_ref, lead=lead):
                rr = pl.multiple_of(r * _CAST_ROWS, _CAST_ROWS)
                wb_ref[pl.ds(rr, _CAST_ROWS), :] = w_ref[lead + (pl.ds(rr, _CAST_ROWS), slice(None))].astype(BF16)
                return carry

            lax.fori_loop(0, rows // _CAST_ROWS, cast, 0)

    @pl.when(valid_ref[i] == 1)
    def _():
        epilogue(lhs, wbs, exs, outs)

    @pl.when(valid_ref[i] == 0)
    def _():
        for o in outs:
            o[...] = jnp.zeros(o.shape, o.dtype)


def _panel_call(epilogue, *, name, grid_n, grid_m, lhs, weights, extras, outs, eid=None, first=None,
                valid=None, rowid=None):
    if eid is None:
        eid = jnp.zeros((grid_m,), jnp.int32)
    if first is None:
        first = jnp.zeros((grid_m,), jnp.int32).at[0].set(1)
    if valid is None:
        valid = jnp.ones((grid_m,), jnp.int32)
    if rowid is None:
        rowid = jnp.zeros((grid_m,), jnp.int32)
    in_arrays, in_specs = [], []
    for arr, blk, imap in list(lhs) + list(weights) + list(extras):
        in_arrays.append(arr)
        in_specs.append(pl.BlockSpec(blk, imap))
    out_shapes = [o[0] for o in outs]
    out_specs = [pl.BlockSpec(o[1], o[2]) for o in outs]
    w_rows = [blk[-2] for _, blk, _ in weights]
    scratch = [pltpu.VMEM((blk[-2], blk[-1]), BF16) for _, blk, _ in weights]
    kern = functools.partial(_panel_kernel, n_lhs=len(lhs), n_w=len(weights), n_ex=len(extras),
                             n_out=len(outs), w_rows=w_rows, epilogue=epilogue)
    res = pl.pallas_call(
        kern,
        grid_spec=pltpu.PrefetchScalarGridSpec(
            num_scalar_prefetch=4, grid=(grid_n, grid_m),
            in_specs=in_specs, out_specs=out_specs, scratch_shapes=scratch),
        out_shape=out_shapes,
        compiler_params=_cparams(("arbitrary", "arbitrary")),
        name=name,
    )(eid, first, valid, rowid, *in_arrays)
    return res


def _ep_plain(lhs, wbs, exs, outs):
    outs[0][...] = _dot(lhs[0][...], wbs[0][...]).astype(outs[0].dtype)


def _ep_ada(lhs, wbs, exs, outs):
    x = _silu(lhs[0][...]).astype(BF16)
    outs[0][...] = _dot(x, wbs[0][...]) + exs[0][0]


def _ep_swiglu(lhs, wbs, exs, outs):
    x = lhs[0][...]
    a1 = _dot(x, wbs[0][...])
    a3 = _dot(x, wbs[1][...])
    outs[0][...] = (_silu(a1) * a3).astype(outs[0].dtype)


def _ep_resid(lhs, wbs, exs, outs):
    outs[0][...] = exs[0][...] + exs[1][0] * _dot(lhs[0][...], wbs[0][...])


def _ep_rowscale(lhs, wbs, exs, outs):
    outs[0][...] = (exs[0][...] * _dot(lhs[0][...], wbs[0][...])).astype(outs[0].dtype)


def _ep_merge(lhs, wbs, exs, outs):
    h = lhs[0][...]
    acc = None
    for n in range(N_BRANCH):
        gate = _sigmoid(_dot(h, wbs[n][...]))
        y = _dot(lhs[1 + n][...], wbs[N_BRANCH + n][...])
        acc = gate * y if acc is None else acc + gate * y
    outs[0][...] = acc.astype(outs[0].dtype)


def _matmul_stream(x, w, layer, col0, n_cols, *, name, tm, tn, out_dtype, grid_m=None):
    M, K = x.shape
    grid_m = M // tm if grid_m is None else grid_m
    jb = col0 // tn
    if w.ndim == 3:
        wspec = (w, (1, K, tn), lambda j, i, e, f, v, r: (layer, 0, jb + j))
    else:
        wspec = (w, (K, tn), lambda j, i, e, f, v, r: (0, jb + j))
    (out,) = _panel_call(
        _ep_plain, name=name, grid_n=n_cols // tn, grid_m=grid_m,
        lhs=[(x, (tm, K), lambda j, i, e, f, v, r: (i, 0))],
        weights=[wspec], extras=[],
        outs=[(jax.ShapeDtypeStruct((grid_m * tm, n_cols), out_dtype), (tm, tn),
               lambda j, i, e, f, v, r: (i, j))])
    return out


def _modulate_kernel(rowid_ref, x_ref, g_ref, sh_ref, sc_ref, *rest, with_router):
    del rowid_ref
    x = x_ref[...]
    y = x * lax.rsqrt(jnp.mean(x * x, axis=-1, keepdims=True) + RMS_EPS)
    h = (y * g_ref[0]) * (1.0 + sc_ref[0]) + sh_ref[0]
    if with_router:
        wr_ref, h_ref, lg_ref = rest
        h_ref[...] = h.astype(BF16)
        lg_ref[...] = _dot_hi(h, wr_ref[...])
    else:
        (h_ref,) = rest
        h_ref[...] = h.astype(BF16)


def _modulate(X, gain, mod3, rowid, layer, k_shift, k_scale, *, tm, grid_m, w_router=None):
    D = X.shape[1]
    with_router = w_router is not None
    in_specs = [
        pl.BlockSpec((tm, D), lambda i, r: (i, 0)),
        pl.BlockSpec((1, 1, D), lambda i, r: (layer, 0, 0)),
        pl.BlockSpec((1, 1, D), lambda i, r: (layer * 16 + r[i], 0, k_shift)),
        pl.BlockSpec((1, 1, D), lambda i, r: (layer * 16 + r[i], 0, k_scale)),
    ]
    args = [X, gain.reshape(gain.shape[0], 1, D), mod3, mod3]
    out_shape = [jax.ShapeDtypeStruct((grid_m * tm, D), BF16)]
    out_specs = [pl.BlockSpec((tm, D), lambda i, r: (i, 0))]
    if with_router:
        in_specs.append(pl.BlockSpec((D, LANES), lambda i, r: (0, 0)))
        args.append(w_router)
        out_shape.append(jax.ShapeDtypeStruct((grid_m * tm, LANES), F32))
        out_specs.append(pl.BlockSpec((tm, LANES), lambda i, r: (i, 0)))
    res = pl.pallas_call(
        functools.partial(_modulate_kernel, with_router=with_router),
        grid_spec=pltpu.PrefetchScalarGridSpec(
            num_scalar_prefetch=1, grid=(grid_m,), in_specs=in_specs, out_specs=out_specs),
        out_shape=out_shape,
        compiler_params=_cparams(("arbitrary",)),
        name="modulate_router" if with_router else "modulate",
    )(rowid, *args)
    return res if with_router else res[0]


def _swap_halves(x):
    lane = lax.broadcasted_iota(jnp.int32, x.shape, x.ndim - 1)
    return jnp.where((lane % 64) < 32, pltpu.roll(x, 96, x.ndim - 1), pltpu.roll(x, 32, x.ndim - 1))


def _rope(x, cos, sin_signed):
    return x * cos + _swap_halves(x) * sin_signed


def _head_rms(x, gain_row):
    return x * lax.rsqrt(jnp.mean(x * x, axis=-1, keepdims=True) + RMS_EPS) * gain_row


ATTN_ALL_ROWS = 256


def _lane_tile_max(s):
    m = s[:, 0:LANES]
    for k in range(1, s.shape[1] // LANES):
        m = jnp.maximum(m, s[:, k * LANES:(k + 1) * LANES])
    return m


def _attn_kernel(*refs, mode, norm, sink, tq, S, L):
    refs = list(refs)
    sink_ref = refs.pop(0) if sink else None
    q_ref, kc_ref, vc_ref = refs[:3]
    refs = refs[3:]
    if mode != 'ctx':
        kl_ref, vl_ref, cos_ref, sin_ref = refs[:4]
        refs = refs[4:]
    if norm:
        qg_ref, kg_ref = refs[:2]
        refs = refs[2:]
    o_ref, qs_ref = refs[:2]
    kp_ref = refs[2] if len(refs) > 2 else None
    ve_ref = refs[3] if len(refs) > 3 else None
    g = pl.program_id(1)
    t = pl.program_id(2)
    rows = 256

    if kp_ref is not None:
        @pl.when(t == 0)
        def _():
            if mode != 'band':
                for r0 in range(0, L, rows):
                    kc = kc_ref[r0:r0 + rows, :].astype(F32)
                    if norm:
                        kc = _head_rms(kc, kg_ref[...])
                    kp_ref[r0:r0 + rows, :] = kc.astype(BF16)
            if mode != 'ctx':
                base = L if mode == 'all' else 0
                for r0 in range(0, S, rows):
                    kl = kl_ref[r0:r0 + rows, :].astype(F32)
                    if norm:
                        kl = _head_rms(kl, kg_ref[...])
                    kl = _rope(kl, cos_ref[r0:r0 + rows, :], sin_ref[r0:r0 + rows, :])
                    kp_ref[base + r0:base + r0 + rows, :] = kl.astype(BF16)
                ve_ref[0:L, 0:HEAD_DIM] = vc_ref[...]
                ve_ref[L:L + S, 0:HEAD_DIM] = vl_ref[...]
                ve_ref[:, HEAD_DIM:2 * HEAD_DIM] = jnp.ones((L + S, HEAD_DIM), BF16)

    scale = (HEAD_DIM ** -0.5) * LOG2E
    if mode != 'ctx':
        t0 = pl.multiple_of(t * tq, tq)
        cos_q = cos_ref[pl.ds(t0, tq), :]
        sin_q = sin_ref[pl.ds(t0, tq), :]
    if mode == 'band':
        nb = 3 * Q_BLOCK
        start = pl.multiple_of(jnp.clip(t * tq - WINDOW, 0, S - nb), Q_BLOCK)
        qpos = t * tq + lax.broadcasted_iota(jnp.int32, (tq, nb), 0)
        kpos = start + lax.broadcasted_iota(jnp.int32, (tq, nb), 1)
        in_band = jnp.abs(qpos - kpos) <= WINDOW

    def scores(j):
        qj = q_ref[:, j * HEAD_DIM:(j + 1) * HEAD_DIM].astype(F32)
        if norm:
            qj = _head_rms(qj, qg_ref[...])
        if mode != 'ctx':
            qj = _rope(qj, cos_q, sin_q)
        qs_ref[j * tq:(j + 1) * tq, :] = (qj * scale).astype(BF16)
        qj = qs_ref[j * tq:(j + 1) * tq, :]
        if mode == 'all':
            half = (L + S) // 2
            return _dot_nt(qj, kp_ref[0:half, :]), _dot_nt(qj, kp_ref[half:L + S, :])
        if mode == 'band':
            s_loc = _dot_nt(qj, kp_ref[pl.ds(start, nb), :])
            return _dot_nt(qj, kc_ref[...]), jnp.where(in_band, s_loc, NEG_BIG)
        return (_dot_nt(qj, kp_ref[...] if norm else kc_ref[...]),)

    def finish(j, sc):
        sink_j = sink_ref[g * KV_GROUP + j] * LOG2E if sink else None
        if mode == 'ctx':
            (s,) = sc
            m = jnp.max(s, axis=-1, keepdims=True)
            if sink:
                m = jnp.maximum(m, sink_j)
            p = jnp.exp2(s - m)
            l = jnp.sum(p, axis=-1, keepdims=True)
            if sink:
                l = l + jnp.exp2(sink_j - m)
            o = _dot(p.astype(BF16), vc_ref[...]) / l
        else:
            s_a, s_b = sc
            m = jnp.max(jnp.maximum(_lane_tile_max(s_a), _lane_tile_max(s_b)), axis=-1, keepdims=True)
            if sink:
                m = jnp.maximum(m, sink_j)
            p_a = jnp.exp2((s_a - m).astype(BF16))
            p_b = jnp.exp2((s_b - m).astype(BF16))
            if mode == 'all':
                half = (L + S) // 2
                oe = _dot(p_a, ve_ref[0:half, :]) + _dot(p_b, ve_ref[half:L + S, :])
            else:
                oe = _dot(p_a, ve_ref[0:L, :]) + _dot(p_b, ve_ref[pl.ds(L + start, nb), :])
            l = oe[:, HEAD_DIM:]
            if sink:
                l = l + jnp.exp2(sink_j - m)
            o = oe[:, :HEAD_DIM] / l
        o_ref[:, j * HEAD_DIM:(j + 1) * HEAD_DIM] = o.astype(o_ref.dtype)

    pending = scores(0)
    for j in range(1, KV_GROUP):
        nxt = scores(j)
        finish(j - 1, pending)
        pending = nxt
    finish(KV_GROUP - 1, pending)


def _attention(z1, cos, sin, q_gain, k_gain, sink_vec, *, mode, B, S, L, qcol, kcol, vcol):
    norm = q_gain is not None
    sink = sink_vec is not None
    lat0 = 0
    ctx0 = (B * S) // L
    if mode == 'ctx':
        tq, nt = L, 1
        q_map = lambda b, g, t: (ctx0 + b, qcol + g)
    else:
        tq = ATTN_ALL_ROWS if mode == 'all' else Q_BLOCK
        nt = S // tq
        q_map = lambda b, g, t: (lat0 + b * nt + t, qcol + g)
    qw = KV_GROUP * HEAD_DIM
    args, in_specs = [], []
    if sink:
        args.append(sink_vec)
        in_specs.append(pl.BlockSpec(memory_space=pltpu.SMEM))
    args += [z1, z1, z1]
    in_specs += [
        pl.BlockSpec((tq, qw), q_map),
        pl.BlockSpec((L, HEAD_DIM), lambda b, g, t: (ctx0 + b, kcol + g)),
        pl.BlockSpec((L, HEAD_DIM), lambda b, g, t: (ctx0 + b, vcol + g)),
    ]
    if mode != 'ctx':
        args += [z1, z1, cos, sin]
        in_specs += [
            pl.BlockSpec((S, HEAD_DIM), lambda b, g, t: (b, kcol + g)),
            pl.BlockSpec((S, HEAD_DIM), lambda b, g, t: (b, vcol + g)),
            pl.BlockSpec((S, HEAD_DIM), lambda b, g, t: (0, 0)),
            pl.BlockSpec((S, HEAD_DIM), lambda b, g, t: (0, 0)),
        ]
    if norm:
        args += [q_gain.reshape(1, HEAD_DIM), k_gain.reshape(1, HEAD_DIM)]
        in_specs += [pl.BlockSpec((1, HEAD_DIM), lambda b, g, t: (0, 0))] * 2
    scratch = [pltpu.VMEM((KV_GROUP * tq, HEAD_DIM), BF16)]
    assert not (mode == 'band' and norm)
    if mode == 'all':
        scratch.append(pltpu.VMEM((L + S, HEAD_DIM), BF16))
        scratch.append(pltpu.VMEM((L + S, 2 * HEAD_DIM), BF16))
    elif mode == 'band':
        scratch.append(pltpu.VMEM((S, HEAD_DIM), BF16))
        scratch.append(pltpu.VMEM((L + S, 2 * HEAD_DIM), BF16))
    elif norm:
        scratch.append(pltpu.VMEM((L, HEAD_DIM), BF16))
    n_rows = B * (L if mode == 'ctx' else S)
    if mode == 'ctx':
        o_map = lambda b, g, t: (b, g)
    else:
        o_map = lambda b, g, t: (b * nt + t, g)
    return pl.pallas_call(
        functools.partial(_attn_kernel, mode=mode, norm=norm, sink=sink, tq=tq, S=S, L=L),
        grid=(B, ATTN_KV_HEADS, nt),
        in_specs=in_specs,
        out_specs=pl.BlockSpec((tq, qw), o_map),
        out_shape=jax.ShapeDtypeStruct((n_rows, BRANCH_WIDTH), BF16),
        scratch_shapes=scratch,
        compiler_params=_cparams(("arbitrary", "arbitrary", "arbitrary")),
        name="attn_%s%s" % (mode, "_sink" if sink else ""),
    )(*args)


CONV_ROWS = 256
HALO = 16


def _gdn_conv_kernel(sfirst_ref, slast_ref, prev_ref, cur_ref, next_ref, w_ref, o_ref):
    r = pl.program_id(0)
    part = pl.program_id(1)
    n = cur_ref.shape[0]
    row = lax.broadcasted_iota(jnp.int32, (n, HEAD_DIM), 0)
    is_first = sfirst_ref[r] == 1
    is_last = slast_ref[r] == 1
    for h in range(GDN_HEADS):
        cols = slice(h * HEAD_DIM, (h + 1) * HEAD_DIM)
        x = cur_ref[:, cols].astype(F32)
        prev_row = jnp.where(is_first, 0.0, prev_ref[:, cols].astype(F32)[HALO - 1:HALO, :])
        next_row = jnp.where(is_last, 0.0, next_ref[:, cols].astype(F32)[0:1, :])
        xm = jnp.where(row == 0, prev_row, pltpu.roll(x, 1, 0))
        xp = jnp.where(row == n - 1, next_row, pltpu.roll(x, n - 1, 0))
        y = _silu(w_ref[0:1, cols] * xm + w_ref[1:2, cols] * x + w_ref[2:3, cols] * xp)
        rs = lax.rsqrt(jnp.sum(y * y, axis=-1, keepdims=True) + RMS_EPS)
        factor = jnp.where(part == 0, rs * (HEAD_DIM ** -0.5), jnp.where(part == 1, rs, 1.0))
        o_ref[:, cols] = y * factor


def _gdn_conv(z1, conv_w, seq_first, seq_last):
    M = z1.shape[0]
    nblk = M // CONV_ROWS
    W = GDN_HEADS * HEAD_DIM
    c0 = 3072 // W
    per = CONV_ROWS // HALO
    nh = M // HALO
    return pl.pallas_call(
        _gdn_conv_kernel,
        grid_spec=pltpu.PrefetchScalarGridSpec(
            num_scalar_prefetch=2, grid=(nblk, 3),
            in_specs=[
                pl.BlockSpec((HALO, W), lambda r, c, a, b: (jnp.maximum(r * per - 1, 0), c0 + c)),
                pl.BlockSpec((CONV_ROWS, W), lambda r, c, a, b: (r, c0 + c)),
                pl.BlockSpec((HALO, W), lambda r, c, a, b: (jnp.minimum((r + 1) * per, nh - 1), c0 + c)),
                pl.BlockSpec((3, W), lambda r, c, a, b: (0, c)),
            ],
            out_specs=pl.BlockSpec((CONV_ROWS, W), lambda r, c, a, b: (r, c))),
        out_shape=jax.ShapeDtypeStruct((M, 3 * W), F32),
        compiler_params=_cparams(("arbitrary", "arbitrary")),
        name="gdn_conv",
    )(seq_first, seq_last, z1, z1, z1, conv_w)


def _chunk_tri(n, upper):
    r = lax.broadcasted_iota(jnp.int32, (n, n), 0)
    c = lax.broadcasted_iota(jnp.int32, (n, n), 1)
    same = (r // CHUNK) == (c // CHUNK)
    tri = (c >= r) if upper else (c <= r)
    return jnp.where(same & tri, 1.0, 0.0).astype(BF16)


def _gdn_gates_kernel(zg_ref, a_ref, dtb_ref, o_ref):
    zg = zg_ref[...]
    n = zg.shape[0]
    lane = lax.broadcasted_iota(jnp.int32, zg.shape, 1)
    beta = _sigmoid(zg)
    t = zg + dtb_ref[...]
    softplus = jnp.maximum(t, 0.0) + jnp.log(1.0 + jnp.exp(-jnp.abs(t)))
    g = jnp.where((lane >= 16) & (lane < 32), a_ref[...] * softplus, 0.0)
    gc_f = _dot_exact_lhs(_chunk_tri(n, False), g)
    gc_b = _dot_exact_lhs(_chunk_tri(n, True), g)
    o_ref[...] = jnp.where(lane < 16, beta, jnp.where(lane < 24, gc_f, gc_b))


def _gdn_gates(zg, a_row, dtb_row):
    M = zg.shape[0]
    tm = 256
    return pl.pallas_call(
        _gdn_gates_kernel,
        grid=(M // tm,),
        in_specs=[pl.BlockSpec((tm, LANES), lambda i: (i, 0)),
                  pl.BlockSpec((1, LANES), lambda i: (0, 0)),
                  pl.BlockSpec((1, LANES), lambda i: (0, 0))],
        out_specs=pl.BlockSpec((tm, LANES), lambda i: (i, 0)),
        out_shape=jax.ShapeDtypeStruct((M, LANES), F32),
        compiler_params=_cparams(("arbitrary",)),
        name="gdn_gates",
    )(zg, a_row, dtb_row)


_INV_BASE = 8


def _unit_tri_inverse(lms, r, c):
    C = lms[0].shape[0]
    eye = jnp.where(r == c, 1.0, 0.0)
    diag = (r // _INV_BASE) == (c // _INV_BASE)
    dms = [jnp.where(diag, lm, 0.0) for lm in lms]
    d16 = [dm.astype(BF16) for dm in dms]
    p16 = [_dot(d, d).astype(BF16) for d in d16]
    xs = [eye - dm for dm in dms]
    rrs = [_dot(jnp.concatenate([p, x.astype(BF16)], axis=0), p) for p, x in zip(p16, xs)]
    xs = [x + rr[C:] for x, rr in zip(xs, rrs)]
    ts = [x + _dot(x.astype(BF16), rr[:C].astype(BF16)) for x, rr in zip(xs, rrs)]
    s = _INV_BASE
    while s < C:
        join = ((r // (2 * s)) == (c // (2 * s))) & ((r // s) != (c // s))
        e16 = [jnp.where(join, lm, 0.0).astype(BF16) for lm in lms]
        t16 = [t.astype(BF16) for t in ts]
        ys = [_dot(e, t).astype(BF16) for e, t in zip(e16, t16)]
        ts = [t - _dot(tb, y) for t, tb, y in zip(ts, t16, ys)]
        s *= 2
    return ts


GDN_PREP_CHUNKS = 8


def _gdn_prep_kernel(q_ref, k_ref, v_ref, g_ref, rf_ref, rb_ref, wq_ref, u_ref, qkkd_ref, gl_ref):
    C = CHUNK
    h = pl.program_id(1)
    r = lax.broadcasted_iota(jnp.int32, (C, C), 0)
    c = lax.broadcasted_iota(jnp.int32, (C, C), 1)
    lane = lax.broadcasted_iota(jnp.int32, (C, LANES), 1)
    row_refs = (rf_ref, rb_ref)
    qs, ks, vs, kks, qks, gts = [], [], [], [], [], []
    for n in range(GDN_PREP_CHUNKS):
        rows = slice(n * C, (n + 1) * C)
        qs.append(q_ref[rows, :])
        ks.append(k_ref[rows, :])
        vs.append(v_ref[rows, :])
        gts.append(g_ref[rows, :])
        kb16 = ks[n].astype(BF16)
        kks.append(_dot_nt(kb16, kb16))
        qks.append(_dot_nt(qs[n].astype(BF16), kb16))

    def column(tile, ch):
        return jnp.sum(jnp.where(lane == ch, tile, 0.0), axis=-1, keepdims=True)

    units = [(n, d) for n in range(GDN_PREP_CHUNKS) for d in range(2)]
    lms, rhs1 = [], []
    for n, d in units:
        beta = column(gts[n], d * GDN_HEADS + h)
        gc = column(gts[n], 16 + d * GDN_HEADS + h)
        grow = row_refs[d][0, n][0:1, :]
        incl = (c <= r) if d == 0 else (c >= r)
        strict = (c < r) if d == 0 else (c > r)
        last = C - 1 if d == 0 else 0
        decay = jnp.where(incl, jnp.exp(jnp.where(incl, gc - grow, 0.0)), 0.0)
        lms.append(jnp.where(strict, beta * kks[n] * decay, 0.0))
        eg = jnp.exp(gc)
        g_last = gc[last:last + 1, :]
        rhs1.append(jnp.concatenate([ks[n] * (beta * eg), vs[n] * beta], axis=-1).astype(BF16))
        wq_ref[d, (2 * n + 1) * C:(2 * n + 2) * C, :] = (qs[n] * eg).astype(BF16)
        kd = ks[n] * jnp.exp(g_last - gc)
        base = n * (C + HEAD_DIM)
        qkkd_ref[d, 0, base:base + C, :] = (qks[n] * decay).astype(BF16)
        qkkd_ref[d, 0, base + C:base + C + HEAD_DIM, :] = jnp.transpose(kd).astype(BF16)
        gl_ref[d, 0, n * 8:(n + 1) * 8, :] = jnp.broadcast_to(jnp.exp(g_last), (8, HEAD_DIM))
    t16 = [t.astype(BF16) for t in _unit_tri_inverse(lms, r, c)]
    wa = [_dot(t, r1) for t, r1 in zip(t16, rhs1)]
    for (n, d), wu in zip(units, wa):
        wq_ref[d, 2 * n * C:(2 * n + 1) * C, :] = wu[:, :HEAD_DIM].astype(BF16)
        u_ref[d, n * C:(n + 1) * C, :] = wu[:, HEAD_DIM:]


def _gdn_prep(qkv, gates, rowb):
    M = qkv.shape[0]
    H = GDN_HEADS
    N = GDN_PREP_CHUNKS
    C = CHUNK
    nch = M // (C * N)
    rb_spec = lambda ch0: pl.BlockSpec((1, N, 8, C), lambda c, h: (ch0 + h, c, 0, 0))
    return pl.pallas_call(
        _gdn_prep_kernel,
        grid=(nch, H),
        in_specs=[
            pl.BlockSpec((N * C, HEAD_DIM), lambda c, h: (c, h)),
            pl.BlockSpec((N * C, HEAD_DIM), lambda c, h: (c, H + h)),
            pl.BlockSpec((N * C, HEAD_DIM), lambda c, h: (c, 2 * H + h)),
            pl.BlockSpec((N * C, LANES), lambda c, h: (c, 0)),
            rb_spec(0), rb_spec(H),
        ],
        out_specs=[
            pl.BlockSpec((2, N * 2 * C, HEAD_DIM), lambda c, h: (0, c, h)),
            pl.BlockSpec((2, N * C, HEAD_DIM), lambda c, h: (0, c, h)),
            pl.BlockSpec((2, 1, N * (C + HEAD_DIM), C), lambda c, h: (0, h, c, 0)),
            pl.BlockSpec((2, 1, N * 8, HEAD_DIM), lambda c, h: (0, h, c, 0)),
        ],
        name="gdn_prep",
        out_shape=[
            jax.ShapeDtypeStruct((2, 2 * M, H * HEAD_DIM), BF16),
            jax.ShapeDtypeStruct((2, M, H * HEAD_DIM), F32),
            jax.ShapeDtypeStruct((2, H, (M // C) * (C + HEAD_DIM), C), BF16),
            jax.ShapeDtypeStruct((2, H, (M // C) * 8, HEAD_DIM), F32),
        ],
        compiler_params=_cparams(("arbitrary", "arbitrary")),
    )(qkv, qkv, qkv, gates, rowb, rowb)


def _gdn_state_kernel(wqf_ref, uf_ref, qkf_ref, glf_ref, wqb_ref, ub_ref, qkb_ref, glb_ref,
                      of_ref, ob_ref, s_ref):
    C = CHUNK
    H = GDN_HEADS

    @pl.when(pl.program_id(1) == 0)
    def _():
        s_ref[...] = jnp.zeros(s_ref.shape, F32)

    dirs = ((wqf_ref, uf_ref, qkf_ref, glf_ref, of_ref), (wqb_ref, ub_ref, qkb_ref, glb_ref, ob_ref))
    units = [(d, h) for d in range(2) for h in range(H)]
    states = [s_ref[d * H + h] for d, h in units]
    for step in range(SCAN_CHUNKS):
        subs = [step if d == 0 else SCAN_CHUNKS - 1 - step for d, _ in units]
        m1s = []
        for (d, h), sub, st in zip(units, subs, states):
            cols = slice(h * HEAD_DIM, (h + 1) * HEAD_DIM)
            m1s.append(_dot(dirs[d][0][0, sub * 2 * C:(sub + 1) * 2 * C, cols], st.astype(BF16)))
        m2s = []
        for (d, h), sub, m1 in zip(units, subs, m1s):
            cols = slice(h * HEAD_DIM, (h + 1) * HEAD_DIM)
            v_new = dirs[d][1][0, sub * C:(sub + 1) * C, cols] - m1[0:C]
            m2s.append(_dot(dirs[d][2][0, h, sub * (C + HEAD_DIM):(sub + 1) * (C + HEAD_DIM), :],
                            v_new.astype(BF16)))
        new_states = []
        for (d, h), sub, st, m1, m2 in zip(units, subs, states, m1s, m2s):
            cols = slice(h * HEAD_DIM, (h + 1) * HEAD_DIM)
            dirs[d][4][sub * C:(sub + 1) * C, cols] = (m1[C:2 * C] + m2[0:C]).astype(BF16)
            new_states.append(st * dirs[d][3][0, h, sub * 8:sub * 8 + 1, :] + m2[C:])
        states = new_states
    for (d, h), st in zip(units, states):
        s_ref[d * H + h] = st


SCAN_CHUNKS = 2


def _seq_chunk_maps(B, S, L):
    C = CHUNK * SCAN_CHUNKS
    lc, sc = L // C, S // C
    ctx0 = (B * S) // C

    def fwd(b, c):
        return jnp.where(c < lc, ctx0 + b * lc + c, b * sc + (c - lc))

    def bwd(b, c):
        return jnp.where(c < lc, ctx0 + b * lc + (lc - 1 - c), b * sc + (sc - 1 - (c - lc)))

    return fwd, bwd, lc + sc


def _gdn_state(wq, u, qkkd, gl, *, B, S, L):
    C = CHUNK
    H = GDN_HEADS
    M = u.shape[1]
    fwd, bwd, nc = _seq_chunk_maps(B, S, L)

    N = SCAN_CHUNKS

    def specs(d, pos):
        return [
            pl.BlockSpec((1, N * 2 * C, H * HEAD_DIM), lambda b, c: (d, pos(b, c), 0)),
            pl.BlockSpec((1, N * C, H * HEAD_DIM), lambda b, c: (d, pos(b, c), 0)),
            pl.BlockSpec((1, H, N * (C + HEAD_DIM), C), lambda b, c: (d, 0, pos(b, c), 0)),
            pl.BlockSpec((1, H, N * 8, HEAD_DIM), lambda b, c: (d, 0, pos(b, c), 0)),
        ]

    return pl.pallas_call(
        _gdn_state_kernel,
        grid=(B, nc),
        in_specs=specs(0, fwd) + specs(1, bwd),
        out_specs=[pl.BlockSpec((N * C, H * HEAD_DIM), lambda b, c: (fwd(b, c), 0)),
                   pl.BlockSpec((N * C, H * HEAD_DIM), lambda b, c: (bwd(b, c), 0))],
        out_shape=[jax.ShapeDtypeStruct((M, H * HEAD_DIM), BF16)] * 2,
        scratch_shapes=[pltpu.VMEM((2 * H, HEAD_DIM, HEAD_DIM), F32)],
        compiler_params=_cparams(("arbitrary", "arbitrary")),
        name="gdn_state",
    )(wq, u, qkkd, gl, wq, u, qkkd, gl)


def _gla_kernel(qf_ref, kf_ref, vf_ref, zf_ref, qb_ref, kb_ref, vb_ref, zb_ref, wgk_ref, bgk_ref,
                of_ref, ob_ref, s_ref):
    C = CHUNK
    r = lax.broadcasted_iota(jnp.int32, (C, C), 0)
    c = lax.broadcasted_iota(jnp.int32, (C, C), 1)

    @pl.when(pl.program_id(1) == 0)
    def _():
        s_ref[...] = jnp.zeros(s_ref.shape, F32)

    dirs = ((qf_ref, kf_ref, vf_ref, zf_ref, of_ref), (qb_ref, kb_ref, vb_ref, zb_ref, ob_ref))
    incls = [(c <= r), (c >= r)]
    nrows = SCAN_CHUNKS * C
    bcums = []
    for d in range(2):
        pre = _dot_hi(dirs[d][3][...], wgk_ref[d]) + bgk_ref[d]
        gk = (jnp.minimum(pre, 0.0) - jnp.log(1.0 + jnp.exp(-jnp.abs(pre)))) * (1.0 / GLA_GATE_NORM)
        bcums.append(_dot_exact_lhs(_chunk_tri(nrows, d == 1), gk))
    units = [(d, h) for d in range(2) for h in range(GLA_HEADS)]
    states = [s_ref[d * GLA_HEADS + h] for d, h in units]
    for step in range(SCAN_CHUNKS):
        scores, inter, upd, dls = [], [], [], []
        for (d, h), st in zip(units, states):
            sub = step if d == 0 else SCAN_CHUNKS - 1 - step
            rows = slice(sub * C, (sub + 1) * C)
            kc = slice(h * GLA_DK, (h + 1) * GLA_DK)
            vc = slice(h * GLA_DV, (h + 1) * GLA_DV)
            mid = C // 2 - 1 if d == 1 else C // 2
            last = 0 if d == 1 else C - 1
            q = dirs[d][0][rows, kc].astype(F32) * (GLA_DK ** -0.5)
            k = dirs[d][1][rows, kc].astype(F32)
            b = bcums[d][sub * C:(sub + 1) * C, kc]
            b_mid = b[mid:mid + 1, :]
            b_last = b[last:last + 1, :]
            scores.append(_dot_nt((q * jnp.exp(b - b_mid)).astype(BF16), (k * jnp.exp(b_mid - b)).astype(BF16)))
            inter.append(_dot((q * jnp.exp(b)).astype(BF16), st.astype(BF16)))
            upd.append(_dot_tn((k * jnp.exp(b_last - b)).astype(BF16), dirs[d][2][rows, vc]))
            dls.append(jnp.transpose(jnp.broadcast_to(jnp.exp(b_last), (GLA_DK, GLA_DK))))
        new_states = []
        for (d, h), st, sc, oi, up, dl in zip(units, states, scores, inter, upd, dls):
            sub = step if d == 0 else SCAN_CHUNKS - 1 - step
            rows = slice(sub * C, (sub + 1) * C)
            vc = slice(h * GLA_DV, (h + 1) * GLA_DV)
            p = jnp.where(incls[d], sc, 0.0).astype(BF16)
            dirs[d][4][rows, vc] = (_dot(p, dirs[d][2][rows, vc]) + oi).astype(BF16)
            new_states.append(st * jnp.concatenate([dl, dl], axis=-1) + up)
        states = new_states
    for (d, h), st in zip(units, states):
        s_ref[d * GLA_HEADS + h] = st


def _gla(z2, zg, wgk_pad, bgk, *, B, S, L):
    C = CHUNK * SCAN_CHUNKS
    M = z2.shape[0]
    fwd, bwd, nc = _seq_chunk_maps(B, S, L)
    kw = GLA_HEADS * GLA_DK
    vw = GLA_HEADS * GLA_DV

    def specs(pos):
        return [
            pl.BlockSpec((C, kw), lambda b, c: (pos(b, c), 0)),
            pl.BlockSpec((C, kw), lambda b, c: (pos(b, c), 1)),
            pl.BlockSpec((C, vw), lambda b, c: (pos(b, c), 1)),
            pl.BlockSpec((C, LANES), lambda b, c: (pos(b, c), 0)),
        ]

    return pl.pallas_call(
        _gla_kernel,
        grid=(B, nc),
        in_specs=specs(fwd) + specs(bwd) + [
            pl.BlockSpec((2, LANES, kw), lambda b, c: (0, 0, 0)),
            pl.BlockSpec((2, 1, kw), lambda b, c: (0, 0, 0)),
        ],
        out_specs=[pl.BlockSpec((C, vw), lambda b, c: (fwd(b, c), 0)),
                   pl.BlockSpec((C, vw), lambda b, c: (bwd(b, c), 0))],
        out_shape=[jax.ShapeDtypeStruct((M, vw), BF16)] * 2,
        scratch_shapes=[pltpu.VMEM((2 * GLA_HEADS, GLA_DK, GLA_DV), F32)],
        compiler_params=_cparams(("arbitrary", "arbitrary")),
        name="gla",
    )(z2, z2, z2, zg, z2, z2, z2, zg, wgk_pad, bgk)


def _headnorm_kernel(of_ref, ob_ref, z_ref, g_ref, o_ref, *, hd):
    n = of_ref.shape[1] // hd
    for h in range(n):
        cols = slice(h * hd, (h + 1) * hd)
        o = of_ref[:, cols].astype(F32) + ob_ref[:, cols].astype(F32)
        y = o * lax.rsqrt(jnp.mean(o * o, axis=-1, keepdims=True) + RMS_EPS) * g_ref[...]
        o_ref[:, cols] = (y * _silu(z_ref[:, cols].astype(F32))).astype(o_ref.dtype)


def _headnorm(of, ob, z, zcol, gain, *, hd, tm=512):
    M, W = of.shape
    return pl.pallas_call(
        functools.partial(_headnorm_kernel, hd=hd),
        grid=(M // tm,),
        in_specs=[pl.BlockSpec((tm, W), lambda i: (i, 0)),
                  pl.BlockSpec((tm, W), lambda i: (i, 0)),
                  pl.BlockSpec((tm, W), lambda i: (i, zcol)),
                  pl.BlockSpec((1, hd), lambda i: (0, 0))],
        out_specs=pl.BlockSpec((tm, W), lambda i: (i, 0)),
        out_shape=jax.ShapeDtypeStruct((M, W), BF16),
        compiler_params=_cparams(("arbitrary",)),
        name="headnorm",
    )(of, ob, z, gain.reshape(1, hd))


def _final_kernel(rowid_ref, *refs, with_y, norm):
    del rowid_ref
    refs = list(refs)
    x = refs.pop(0)[...]
    if with_y:
        y0_ref, y1_ref, gt_ref = refs[:3]
        refs = refs[3:]
        x = x + gt_ref[0] * (y0_ref[...].astype(F32) + y1_ref[...].astype(F32))
    if norm:
        g_ref = refs.pop(0)
        x = x * lax.rsqrt(jnp.mean(x * x, axis=-1, keepdims=True) + RMS_EPS) * g_ref[...]
    refs[0][...] = x


def _final(X, ys, mod3, rowid, layer, k_gate, g_final, *, tm, grid_m):
    D = X.shape[1]
    row_spec = pl.BlockSpec((tm, D), lambda i, r: (i, 0))
    args, in_specs = [X], [row_spec]
    if ys is not None:
        args += [ys[0], ys[1], mod3]
        in_specs += [row_spec, row_spec,
                     pl.BlockSpec((1, 1, D), lambda i, r: (layer * 16 + r[i], 0, k_gate))]
    if g_final is not None:
        args.append(g_final.reshape(1, D))
        in_specs.append(pl.BlockSpec((1, D), lambda i, r: (0, 0)))
    return pl.pallas_call(
        functools.partial(_final_kernel, with_y=ys is not None, norm=g_final is not None),
        grid_spec=pltpu.PrefetchScalarGridSpec(
            num_scalar_prefetch=1, grid=(grid_m,), in_specs=in_specs, out_specs=row_spec),
        out_shape=jax.ShapeDtypeStruct((grid_m * tm, D), F32),
        compiler_params=_cparams(("arbitrary",)),
        name="final",
    )(rowid, *args)


def _rope_tables(S):
    rows = S // GRID_W
    row = jnp.repeat(jnp.arange(rows, dtype=F32), GRID_W)
    col = jnp.tile(jnp.arange(GRID_W, dtype=F32), rows)
    inv_freq = ROPE_THETA ** (-jnp.arange(ROPE_FREQS, dtype=F32) / ROPE_FREQS)
    ar = row[:, None] * inv_freq
    ac = col[:, None] * inv_freq
    cos = jnp.concatenate([jnp.cos(ar), jnp.cos(ar), jnp.cos(ac), jnp.cos(ac)], axis=-1)
    sin = jnp.concatenate([-jnp.sin(ar), jnp.sin(ar), -jnp.sin(ac), jnp.sin(ac)], axis=-1)
    return cos, sin


def _stream_rowid(B, S, L, tm, grid_m):
    i = jnp.arange(grid_m, dtype=jnp.int32) * tm
    return jnp.where(i < B * S, i // S, B).astype(jnp.int32)


def _moe_plan(logits, T, tm, n_tiles):
    top_val, top_idx = lax.top_k(logits, TOP_K)
    top_w = jax.nn.softmax(top_val, axis=-1)
    e_flat = top_idx.reshape(-1).astype(jnp.int32)
    w_flat = top_w.reshape(-1)
    n_assign = T * TOP_K
    order = jnp.argsort(e_flat, stable=True).astype(jnp.int32)
    inv = jnp.argsort(order).astype(jnp.int32)
    experts = jnp.arange(N_EXPERTS, dtype=jnp.int32)
    sizes = jnp.sum((e_flat[:, None] == experts[None, :]).astype(jnp.int32), axis=0)
    start = jnp.cumsum(sizes) - sizes
    padded = ((sizes + tm - 1) // tm) * tm
    pend = jnp.cumsum(padded)
    pstart = pend - padded
    dest = pstart[e_flat] + inv - start[e_flat]
    tile0 = jnp.arange(n_tiles, dtype=jnp.int32) * tm
    tile_e = jnp.minimum(jnp.searchsorted(pend, tile0, side='right'), N_EXPERTS - 1).astype(jnp.int32)
    valid = (tile0 < pend[-1]).astype(jnp.int32)
    row = jnp.arange(n_tiles * tm, dtype=jnp.int32)
    row_e = jnp.repeat(tile_e, tm)
    row_rank = row - pstart[row_e]
    row_ok = (row_rank < sizes[row_e]) & (jnp.repeat(valid, tm) == 1)
    row_a = order[jnp.clip(start[row_e] + row_rank, 0, n_assign - 1)]
    row_token = jnp.where(row_ok, row_a // TOP_K, row % T).astype(jnp.int32)
    row_w = jnp.where(row_ok, w_flat[row_a], 0.0)
    last_e = tile_e[jnp.maximum(pend[-1] // tm - 1, 0)]
    tile_e = jnp.where(valid == 1, tile_e, last_e)
    first = jnp.concatenate([jnp.ones((1,), jnp.int32), (tile_e[1:] != tile_e[:-1]).astype(jnp.int32)])
    return row_token, row_w, dest.reshape(T, TOP_K), tile_e, first, valid


def kernel(x, c, ctx, c_ctx, w_ada, b_ada, g_norm_mix, g_norm_ffn, w_in, a_q_gain, a_k_gain, b_sink,
           gdn_conv, gdn_a_log, gdn_dt_bias, gdn_norm_gain, gla_w_gk, gla_b_gk, gla_norm_gain,
           w_branch, w_merge, w_out, w1_dense, w3_dense, w2_dense, w_router, w1_moe, w3_moe, w2_moe,
           g_final):
    B, S, D = x.shape
    L = ctx.shape[1]
    depth = w_in.shape[0]
    n_lat = B * S
    M = n_lat + B * L
    TM = 512
    TM_BIG = 1024
    assert M % (CHUNK * GDN_PREP_CHUNKS) == 0 and L % (CHUNK * SCAN_CHUNKS) == 0
    assert D == D_MODEL and S % CONV_ROWS == 0 and L % CONV_ROWS == 0 and n_lat % TM == 0
    assert (B * L) % TM == 0 and S % TM == 0 and S >= 3 * Q_BLOCK and n_lat % L == 0 and B <= 15

    cond = jnp.zeros((16, D), F32).at[:B].set(c).at[B].set(c_ctx)
    cond2 = jnp.concatenate([cond] * depth, axis=0)
    (mod,) = _panel_call(
        _ep_ada, name="ada", grid_n=(6 * D) // 512, grid_m=depth,
        lhs=[(cond2, (16, D), lambda j, i, e, f, v, r: (i, 0))],
        weights=[(w_ada, (1, D, 512), lambda j, i, e, f, v, r: (e[i], 0, j))],
        extras=[(b_ada.reshape(depth, 1, 6 * D), (1, 1, 512), lambda j, i, e, f, v, r: (e[i], 0, j))],
        outs=[(jax.ShapeDtypeStruct((16 * depth, 6 * D), F32), (16, 512), lambda j, i, e, f, v, r: (i, j))],
        eid=jnp.arange(depth, dtype=jnp.int32), first=jnp.ones((depth,), jnp.int32))
    mod3 = mod.reshape(16 * depth, 1, 6 * D)

    cos, sin = _rope_tables(S)
    X = jnp.concatenate([x.reshape(n_lat, D), ctx.reshape(B * L, D)], axis=0)
    gm_all = M // TM
    gm_lat = n_lat // TM
    rowid = _stream_rowid(B, S, L, TM, gm_all)

    blk = jnp.arange(M // CONV_ROWS, dtype=jnp.int32) * CONV_ROWS
    seq_len = jnp.where(blk < n_lat, S, L)
    seq_off = jnp.where(blk < n_lat, blk, blk - n_lat)
    seq_first = (seq_off % seq_len == 0).astype(jnp.int32)
    seq_last = ((seq_off + CONV_ROWS) % seq_len == 0).astype(jnp.int32)

    out = None
    for layer in range(depth):
        last = layer == depth - 1
        gm = gm_lat if last else gm_all
        h = _modulate(X, g_norm_mix, mod3, rowid, layer, 0, 1, tm=TM, grid_m=M // TM)
        tm_z = TM_BIG if M % TM_BIG == 0 else TM
        z1 = _matmul_stream(h, w_in, layer, 0, Z1_WIDTH, name="proj_z1", tm=tm_z, tn=1024, out_dtype=BF16)
        w_d = lax.slice(w_in, (layer, 0, COL_D_Q), (layer + 1, D, COL_D_GATE)).reshape(D, Z2_WIDTH)
        z2 = _matmul_stream(h, w_d, layer, 0, Z2_WIDTH, name="proj_z2", tm=tm_z, tn=1024, out_dtype=BF16)
        w_g = jnp.concatenate([
            lax.slice(w_in, (layer, 0, COL_C_BETA), (layer + 1, D, COL_D_Q)).reshape(D, COL_D_Q - COL_C_BETA),
            lax.slice(w_in, (layer, 0, COL_D_GATE), (layer + 1, D, PROJ_WIDTH)).reshape(D, PROJ_WIDTH - COL_D_GATE),
            jnp.zeros((D, LANES - 64), F32)], axis=1)
        zg = _matmul_stream(h, w_g, layer, 0, LANES, name="proj_gates", tm=TM, tn=LANES, out_dtype=F32)

        oa = _attention(z1, cos, sin, a_q_gain[layer], a_k_gain[layer], None, mode='all',
                        B=B, S=S, L=L, qcol=0, kcol=8, vcol=10)
        ob = _attention(z1, cos, sin, None, None, b_sink[layer], mode='band',
                        B=B, S=S, L=L, qcol=3, kcol=20, vcol=22)
        if not last:
            oa_c = _attention(z1, cos, sin, a_q_gain[layer], a_k_gain[layer], None, mode='ctx',
                              B=B, S=S, L=L, qcol=0, kcol=8, vcol=10)
            ob_c = _attention(z1, cos, sin, None, None, b_sink[layer], mode='ctx',
                              B=B, S=S, L=L, qcol=3, kcol=20, vcol=22)
            oa = jnp.concatenate([oa, oa_c], axis=0)
            ob = jnp.concatenate([ob, ob_c], axis=0)

        qkv = _gdn_conv(z1, gdn_conv[layer], seq_first, seq_last)
        a_row = jnp.zeros((1, LANES), F32).at[0, 16:32].set(-jnp.exp(gdn_a_log[layer]).reshape(-1))
        dtb_row = jnp.zeros((1, LANES), F32).at[0, 16:32].set(gdn_dt_bias[layer].reshape(-1))
        gates = _gdn_gates(zg, a_row, dtb_row)
        gt = gates[:, 16:32].T
        rowb = jnp.broadcast_to(gt.reshape(16, M // CHUNK, 1, CHUNK), (16, M // CHUNK, 8, CHUNK))
        wq, u, qkkd, gl = _gdn_prep(qkv, gates, rowb)
        oc_f, oc_b = _gdn_state(wq, u, qkkd, gl, B=B, S=S, L=L)
        oc = _headnorm(oc_f, oc_b, z1, 6144 // BRANCH_WIDTH, gdn_norm_gain[layer], hd=HEAD_DIM)

        wgk_pad = jnp.zeros((2, LANES, GLA_HEADS * GLA_DK), F32)
        wgk_pad = wgk_pad.at[0, 32:48].set(gla_w_gk[layer, 0]).at[1, 48:64].set(gla_w_gk[layer, 1])
        od_f, od_b = _gla(z2, zg, wgk_pad, gla_b_gk[layer].reshape(2, 1, -1), B=B, S=S, L=L)
        od = _headnorm(od_f, od_b, z2, 2048 // BRANCH_WIDTH, gla_norm_gain[layer], hd=GLA_DV)

        tn_m = 256
        branches = [oa, ob, oc, od]
        lhs = [(h, (TM, D), lambda j, i, e, f, v, r: (i, 0))]
        lhs += [(o, (TM, BRANCH_WIDTH), lambda j, i, e, f, v, r: (i, 0)) for o in branches]
        wts = [(w_merge, (1, D, tn_m), lambda j, i, e, f, v, r, n=n: (layer, 0, n * (D // tn_m) + j))
               for n in range(N_BRANCH)]
        wts += [(w_branch, (1, 1, BRANCH_WIDTH, tn_m), lambda j, i, e, f, v, r, n=n: (layer, n, 0, j))
                for n in range(N_BRANCH)]
        (merged,) = _panel_call(
            _ep_merge, name="merge", grid_n=D // tn_m, grid_m=gm, lhs=lhs, weights=wts, extras=[],
            outs=[(jax.ShapeDtypeStruct((gm * TM, D), BF16), (TM, tn_m), lambda j, i, e, f, v, r: (i, j))])

        def resid(name, xin, w, wmap, kdim, k_gate, tn):
            (res,) = _panel_call(
                _ep_resid, name=name, grid_n=D // tn, grid_m=gm,
                lhs=[(xin, (TM, kdim), lambda j, i, e, f, v, r: (i, 0))],
                weights=[(w, (1, kdim, tn), wmap)],
                extras=[(X, (TM, tn), lambda j, i, e, f, v, r: (i, j)),
                        (mod3, (1, 1, tn), lambda j, i, e, f, v, r: (layer * 16 + r[i], 0, k_gate * (D // tn) + j))],
                outs=[(jax.ShapeDtypeStruct((gm * TM, D), F32), (TM, tn), lambda j, i, e, f, v, r: (i, j))],
                rowid=rowid[:gm])
            return res

        X = resid("out_proj", merged, w_out,lambda j, i, e, f, v, r: (layer, 0, j), D, 2, 512)

        if layer % 2 == 0:
            li = layer // 2
            h2 = _modulate(X, g_norm_ffn, mod3, rowid, layer, 3, 4, tm=TM, grid_m=gm)
            d_ff = w1_dense.shape[2]
            tm_u = TM_BIG if (gm * TM) % TM_BIG == 0 else TM
            (uu,) = _panel_call(
                _ep_swiglu, name="dense_up", grid_n=d_ff // 512, grid_m=(gm * TM) // tm_u,
                lhs=[(h2, (tm_u, D), lambda j, i, e, f, v, r: (i, 0))],
                weights=[(w1_dense, (1, D, 512), lambda j, i, e, f, v, r: (li, 0, j)),
                         (w3_dense, (1, D, 512), lambda j, i, e, f, v, r: (li, 0, j))],
                extras=[],
                outs=[(jax.ShapeDtypeStruct((gm * TM, d_ff), BF16), (tm_u, 512), lambda j, i, e, f, v, r: (i, j))])
            X = resid("dense_down", uu, w2_dense, lambda j, i, e, f, v, r: (li, 0, j), d_ff, 5, 512)
            if last:
                out = _final(X, None, mod3, rowid, layer, 5, g_final, tm=TM, grid_m=n_lat // TM)
        else:
            li = layer // 2
            T = gm * TM
            wr_pad = jnp.zeros((D, LANES), F32).at[:, :N_EXPERTS].set(w_router[li])
            h2, logits = _modulate(X, g_norm_ffn, mod3, rowid, layer, 3, 4, tm=TM, grid_m=gm, w_router=wr_pad)
            tm_e = 512
            n_tiles = (T * TOP_K) // tm_e + N_EXPERTS
            row_token, row_w, dest, tile_e, first, valid = _moe_plan(logits[:, :N_EXPERTS], T, tm_e, n_tiles)
            xg = h2.at[row_token].get(mode="promise_in_bounds")
            d_ffe = w1_moe.shape[3]
            (uu,) = _panel_call(
                _ep_swiglu, name="moe_up", grid_n=d_ffe // 1024, grid_m=n_tiles,
                lhs=[(xg, (tm_e, D), lambda j, i, e, f, v, r: (i, 0))],
                weights=[(w1_moe, (1, 1, D, 1024), lambda j, i, e, f, v, r: (li, e[i], 0, j)),
                         (w3_moe, (1, 1, D, 1024), lambda j, i, e, f, v, r: (li, e[i], 0, j))],
                extras=[],
                outs=[(jax.ShapeDtypeStruct((n_tiles * tm_e, d_ffe), BF16), (tm_e, 1024),
                       lambda j, i, e, f, v, r: (i, j))],
                eid=tile_e, first=first, valid=valid)
            tm_d = 512
            tn_d = 512
            rep = tm_e // tm_d
            first_d = jnp.repeat(first, rep) * (jnp.arange(n_tiles * rep, dtype=jnp.int32) % rep == 0)
            (yo,) = _panel_call(
                _ep_rowscale, name="moe_down", grid_n=D // tn_d, grid_m=n_tiles * rep,
                lhs=[(uu, (tm_d, d_ffe), lambda j, i, e, f, v, r: (i, 0))],
                weights=[(w2_moe, (1, 1, d_ffe, tn_d), lambda j, i, e, f, v, r: (li, e[i], 0, j))],
                extras=[(row_w.reshape(-1, 1), (tm_d, 1), lambda j, i, e, f, v, r: (i, 0))],
                outs=[(jax.ShapeDtypeStruct((n_tiles * tm_e, D), BF16), (tm_d, tn_d),
                       lambda j, i, e, f, v, r: (i, j))],
                eid=jnp.repeat(tile_e, rep), first=first_d.astype(jnp.int32), valid=jnp.repeat(valid, rep))
            ys = (yo.at[dest[:, 0]].get(mode="promise_in_bounds"),
                  yo.at[dest[:, 1]].get(mode="promise_in_bounds"))
            res = _final(X, ys, mod3, rowid, layer, 5, g_final if last else None, tm=TM, grid_m=gm)
            if last:
                out = res
            else:
                X = res
    return out.reshape(B, S, D)
```

```python
import functools

import jax
import jax.numpy as jnp
from jax import lax
from jax.experimental import pallas as pl
from jax.experimental.pallas import tpu as pltpu

F32 = jnp.float32
BF16 = jnp.bfloat16

D_MODEL = 2048
HEAD_DIM = 128
GRID_W = 64
ROPE_THETA = 10000.0
ROPE_FREQS = HEAD_DIM // 4
RMS_EPS = 1e-6
BRANCH_WIDTH = D_MODEL // 2
KV_GROUP = 4
ATTN_KV_HEADS = 2
WINDOW = 128
Q_BLOCK = 128
GDN_HEADS = 8
GLA_HEADS = 4
GLA_DK = 128
GLA_DV = 256
GLA_RANK = 16
GLA_GATE_NORM = 16.0
CHUNK = 64
N_EXPERTS = 8
TOP_K = 2
N_BRANCH = 4
NEG_BIG = -1e30
LOG2E = 1.4426950408889634

COL_C_BETA = 7168
COL_D_Q = 7200
COL_D_GATE = 10272
PROJ_WIDTH = 10304
Z1_WIDTH = 7168
Z2_WIDTH = 3072

VMEM_LIMIT = 56 * 1024 * 1024
LANES = 128


def _cparams(sem):
    return pltpu.CompilerParams(dimension_semantics=sem, vmem_limit_bytes=VMEM_LIMIT)


def _silu(x):
    return x * (1.0 / (1.0 + jnp.exp(-x)))


def _sigmoid(x):
    return 1.0 / (1.0 + jnp.exp(-x))


def _split3(x):
    x1 = x.astype(BF16)
    r1 = x - x1.astype(F32)
    x2 = r1.astype(BF16)
    x3 = (r1 - x2.astype(F32)).astype(BF16)
    return x1, x2, x3


def _dot(a, b):
    return jnp.dot(a, b, preferred_element_type=F32)


def _dot_nt(a, b):
    return lax.dot_general(a, b, (((1,), (1,)), ((), ())), preferred_element_type=F32)


def _dot_tn(a, b):
    return lax.dot_general(a, b, (((0,), (0,)), ((), ())), preferred_element_type=F32)


def _dot_exact_lhs(tri_bf16, x):
    x1, x2, x3 = _split3(x)
    return _dot(tri_bf16, x1) + _dot(tri_bf16, x2) + _dot(tri_bf16, x3)


def _dot_hi(a, b):
    a1 = a.astype(BF16)
    a2 = (a - a1.astype(F32)).astype(BF16)
    b1 = b.astype(BF16)
    b2 = (b - b1.astype(F32)).astype(BF16)
    return _dot(a1, b1) + _dot(a1, b2) + _dot(a2, b1)


_CAST_ROWS = 256


def _panel_kernel(eid_ref, first_ref, valid_ref, rowid_ref, *refs, n_lhs, n_w, n_ex, n_out,
                  w_rows, epilogue):
    del eid_ref, rowid_ref
    lhs = refs[:n_lhs]
    ws = refs[n_lhs:n_lhs + n_w]
    exs = refs[n_lhs + n_w:n_lhs + n_w + n_ex]
    outs = refs[n_lhs + n_w + n_ex:n_lhs + n_w + n_ex + n_out]
    wbs = refs[n_lhs + n_w + n_ex + n_out:]
    i = pl.program_id(1)

    @pl.when(first_ref[i] == 1)
    def _():
        for w_ref, wb_ref, rows in zip(ws, wbs, w_rows):
            lead = (0,) * (len(w_ref.shape) - 2)
            step = min(_CAST_ROWS, rows)
            assert rows % step == 0

            def cast(r, carry, w_ref=w_ref, wb_ref=wb_ref, lead=lead, step=step):
                rr = pl.multiple_of(r * step, step)
                wb_ref[pl.ds(rr, step), :] = w_ref[lead + (pl.ds(rr, step), slice(None))].astype(BF16)
                return carry

            lax.fori_loop(0, rows // step, cast, 0)

    @pl.when(valid_ref[i] == 1)
    def _():
        epilogue(lhs, wbs, exs, outs)

    @pl.when(valid_ref[i] == 0)
    def _():
        for o in outs:
            o[...] = jnp.zeros(o.shape, o.dtype)


def _panel_call(epilogue, *, name, grid_n, grid_m, lhs, weights, extras, outs, eid=None, first=None,
                valid=None, rowid=None):
    if eid is None:
        eid = jnp.zeros((grid_m,), jnp.int32)
    if first is None:
        first = jnp.zeros((grid_m,), jnp.int32).at[0].set(1)
    if valid is None:
        valid = jnp.ones((grid_m,), jnp.int32)
    if rowid is None:
        rowid = jnp.zeros((grid_m,), jnp.int32)
    in_arrays, in_specs = [], []
    for arr, blk, imap in list(lhs) + list(weights) + list(extras):
        in_arrays.append(arr)
        in_specs.append(pl.BlockSpec(blk, imap))
    out_shapes = [o[0] for o in outs]
    out_specs = [pl.BlockSpec(o[1], o[2]) for o in outs]
    w_rows = [blk[-2] for _, blk, _ in weights]
    scratch = [pltpu.VMEM((blk[-2], blk[-1]), BF16) for _, blk, _ in weights]
    kern = functools.partial(_panel_kernel, n_lhs=len(lhs), n_w=len(weights), n_ex=len(extras),
                             n_out=len(outs), w_rows=w_rows, epilogue=epilogue)
    res = pl.pallas_call(
        kern,
        grid_spec=pltpu.PrefetchScalarGridSpec(
            num_scalar_prefetch=4, grid=(grid_n, grid_m),
            in_specs=in_specs, out_specs=out_specs, scratch_shapes=scratch),
        out_shape=out_shapes,
        compiler_params=_cparams(("arbitrary", "arbitrary")),
        name=name,
    )(eid, first, valid, rowid, *in_arrays)
    return res


def _ep_ada(lhs, wbs, exs, outs):
    x = _silu(lhs[0][...]).astype(BF16)
    outs[0][...] = _dot(x, wbs[0][...]) + exs[0][0]


def _ep_swiglu(lhs, wbs, exs, outs):
    x = lhs[0][...]
    a1 = _dot(x, wbs[0][...])
    a3 = _dot(x, wbs[1][...])
    outs[0][...] = (_silu(a1) * a3).astype(outs[0].dtype)


def _ep_resid(lhs, wbs, exs, outs):
    outs[0][...] = exs[0][...] + exs[1][0] * _dot(lhs[0][...], wbs[0][...])


def _ep_rowscale(lhs, wbs, exs, outs):
    outs[0][...] = (exs[0][...] * _dot(lhs[0][...], wbs[0][...])).astype(outs[0].dtype)


def _ep_merge(lhs, wbs, exs, outs):
    h = lhs[0][...]
    acc = None
    for n in range(N_BRANCH):
        gate = _sigmoid(_dot(h, wbs[n][...]))
        y = _dot(lhs[1 + n][...], wbs[N_BRANCH + n][...])
        acc = gate * y if acc is None else acc + gate * y
    outs[0][...] = acc.astype(outs[0].dtype)


def _ep_plain_nt(lhs, wbs, exs, outs):
    outs[0][...] = _dot_nt(lhs[0][...], wbs[0][...]).astype(outs[0].dtype)


def _matmul_stream(x, wt, layer, row0, n_cols, *, name, tm, tn, out_dtype, grid_m=None):
    M, K = x.shape
    grid_m = M // tm if grid_m is None else grid_m
    jb = row0 // tn
    if wt.ndim == 3:
        wspec = (wt, (1, tn, K), lambda j, i, e, f, v, r: (layer, jb + j, 0))
    else:
        wspec = (wt, (tn, K), lambda j, i, e, f, v, r: (jb + j, 0))
    (out,) = _panel_call(
        _ep_plain_nt, name=name, grid_n=n_cols // tn, grid_m=grid_m,
        lhs=[(x, (tm, K), lambda j, i, e, f, v, r: (i, 0))],
        weights=[wspec], extras=[],
        outs=[(jax.ShapeDtypeStruct((grid_m * tm, n_cols), out_dtype), (tm, tn),
               lambda j, i, e, f, v, r: (i, j))])
    return out


def _modulate_kernel(rowid_ref, x_ref, g_ref, sh_ref, sc_ref, *rest, with_router):
    del rowid_ref
    x = x_ref[...]
    y = x * lax.rsqrt(jnp.mean(x * x, axis=-1, keepdims=True) + RMS_EPS)
    h = (y * g_ref[0]) * (1.0 + sc_ref[0]) + sh_ref[0]
    if with_router:
        wr_ref, h_ref, lg_ref = rest
        h_ref[...] = h.astype(BF16)
        lg_ref[...] = _dot_hi(h, wr_ref[...])
    else:
        (h_ref,) = rest
        h_ref[...] = h.astype(BF16)


def _modulate(X, gain, mod3, rowid, layer, k_shift, k_scale, *, tm, grid_m, w_router=None):
    D = X.shape[1]
    with_router = w_router is not None
    in_specs = [
        pl.BlockSpec((tm, D), lambda i, r: (i, 0)),
        pl.BlockSpec((1, 1, D), lambda i, r: (layer, 0, 0)),
        pl.BlockSpec((1, 1, D), lambda i, r: (layer * 16 + r[i], 0, k_shift)),
        pl.BlockSpec((1, 1, D), lambda i, r: (layer * 16 + r[i], 0, k_scale)),
    ]
    args = [X, gain.reshape(gain.shape[0], 1, D), mod3, mod3]
    out_shape = [jax.ShapeDtypeStruct((grid_m * tm, D), BF16)]
    out_specs = [pl.BlockSpec((tm, D), lambda i, r: (i, 0))]
    if with_router:
        in_specs.append(pl.BlockSpec((D, LANES), lambda i, r: (0, 0)))
        args.append(w_router)
        out_shape.append(jax.ShapeDtypeStruct((grid_m * tm, LANES), F32))
        out_specs.append(pl.BlockSpec((tm, LANES), lambda i, r: (i, 0)))
    res = pl.pallas_call(
        functools.partial(_modulate_kernel, with_router=with_router),
        grid_spec=pltpu.PrefetchScalarGridSpec(
            num_scalar_prefetch=1, grid=(grid_m,), in_specs=in_specs, out_specs=out_specs),
        out_shape=out_shape,
        compiler_params=_cparams(("arbitrary",)),
        name="modulate_router" if with_router else "modulate",
    )(rowid, *args)
    return res if with_router else res[0]


def _swap_halves(x):
    lane = lax.broadcasted_iota(jnp.int32, x.shape, x.ndim - 1)
    return jnp.where((lane % 64) < 32, pltpu.roll(x, 96, x.ndim - 1), pltpu.roll(x, 32, x.ndim - 1))


def _rope(x, cos, sin_signed):
    return x * cos + _swap_halves(x) * sin_signed


def _head_rms(x, gain_row):
    return x * lax.rsqrt(jnp.mean(x * x, axis=-1, keepdims=True) + RMS_EPS) * gain_row


ATTN_ALL_ROWS = 256
ATTN_BAND_ROWS = 2 * Q_BLOCK


def _lane_tile_max(s):
    m = s[:, 0:LANES]
    for k in range(1, s.shape[1] // LANES):
        m = jnp.maximum(m, s[:, k * LANES:(k + 1) * LANES])
    return m


def _attn_kernel(*refs, mode, norm, sink, tq, S, L):
    refs = list(refs)
    sink_ref = refs.pop(0) if sink else None
    q_ref, kc_ref, vc_ref = refs[:3]
    refs = refs[3:]
    if mode != 'ctx':
        kl_ref, vl_ref, cos_ref, sin_ref = refs[:4]
        refs = refs[4:]
    if norm:
        qg_ref, kg_ref = refs[:2]
        refs = refs[2:]
    o_ref, qs_ref = refs[:2]
    kp_ref = refs[2] if len(refs) > 2 else None
    ve_ref = refs[3] if len(refs) > 3 else None
    g = pl.program_id(1)
    t = pl.program_id(2)
    rows = 256

    if kp_ref is not None:
        @pl.when(t == 0)
        def _():
            if mode != 'band':
                for r0 in range(0, L, rows):
                    kc = kc_ref[r0:r0 + rows, :].astype(F32)
                    if norm:
                        kc = _head_rms(kc, kg_ref[...])
                    kp_ref[r0:r0 + rows, :] = kc.astype(BF16)
            if mode != 'ctx':
                base = L if mode == 'all' else 0
                for r0 in range(0, S, rows):
                    kl = kl_ref[r0:r0 + rows, :].astype(F32)
                    if norm:
                        kl = _head_rms(kl, kg_ref[...])
                    kl = _rope(kl, cos_ref[r0:r0 + rows, :], sin_ref[r0:r0 + rows, :])
                    kp_ref[base + r0:base + r0 + rows, :] = kl.astype(BF16)
                ve_ref[0:L, 0:HEAD_DIM] = vc_ref[...]
                ve_ref[L:L + S, 0:HEAD_DIM] = vl_ref[...]
                ve_ref[:, HEAD_DIM:2 * HEAD_DIM] = jnp.ones((L + S, HEAD_DIM), BF16)

    scale = (HEAD_DIM ** -0.5) * LOG2E
    if mode != 'ctx':
        t0 = pl.multiple_of(t * tq, tq)
        cos_q = cos_ref[pl.ds(t0, tq), :]
        sin_q = sin_ref[pl.ds(t0, tq), :]
    rq = Q_BLOCK if mode == 'band' else tq
    nsub = tq // rq
    if mode == 'band':
        nb = 3 * Q_BLOCK
        starts, in_bands = [], []
        for sb in range(nsub):
            blk0 = t * tq + sb * rq
            starts.append(pl.multiple_of(jnp.clip(blk0 - WINDOW, 0, S - nb), Q_BLOCK))
            qpos = blk0 + lax.broadcasted_iota(jnp.int32, (rq, nb), 0)
            kpos = starts[sb] + lax.broadcasted_iota(jnp.int32, (rq, nb), 1)
            in_bands.append(jnp.abs(qpos - kpos) <= WINDOW)

    def scores(item):
        sb, j = item
        rows = slice(sb * rq, (sb + 1) * rq)
        qj = q_ref[rows, j * HEAD_DIM:(j + 1) * HEAD_DIM].astype(F32)
        if norm:
            qj = _head_rms(qj, qg_ref[...])
        if mode != 'ctx':
            qj = _rope(qj, cos_q[rows], sin_q[rows])
        base = (j * nsub + sb) * rq
        qs_ref[base:base + rq, :] = (qj * scale).astype(BF16)
        qj = qs_ref[base:base + rq, :]
        if mode == 'all':
            half = (L + S) // 2
            return _dot_nt(qj, kp_ref[0:half, :]), _dot_nt(qj, kp_ref[half:L + S, :])
        if mode == 'band':
            s_loc = _dot_nt(qj, kp_ref[pl.ds(starts[sb], nb), :])
            return _dot_nt(qj, kc_ref[...]), jnp.where(in_bands[sb], s_loc, NEG_BIG)
        return (_dot_nt(qj, kp_ref[...] if norm else kc_ref[...]),)

    def finish(item, sc):
        sb, j = item
        rows = slice(sb * rq, (sb + 1) * rq)
        sink_j = sink_ref[g * KV_GROUP + j] * LOG2E if sink else None
        if mode == 'ctx':
            (s,) = sc
            m = jnp.max(s, axis=-1, keepdims=True)
            if sink:
                m = jnp.maximum(m, sink_j)
            p = jnp.exp2(s - m)
            l = jnp.sum(p, axis=-1, keepdims=True)
            if sink:
                l = l + jnp.exp2(sink_j - m)
            o = _dot(p.astype(BF16), vc_ref[...]) / l
        else:
            s_a, s_b = sc
            m = jnp.max(jnp.maximum(_lane_tile_max(s_a), _lane_tile_max(s_b)), axis=-1, keepdims=True)
            if sink:
                m = jnp.maximum(m, sink_j)
            p_a = jnp.exp2((s_a - m).astype(BF16))
            p_b = jnp.exp2((s_b - m).astype(BF16))
            if mode == 'all':
                half = (L + S) // 2
                oe = _dot(p_a, ve_ref[0:half, :]) + _dot(p_b, ve_ref[half:L + S, :])
            else:
                oe = _dot(p_a, ve_ref[0:L, :]) + _dot(p_b, ve_ref[pl.ds(L + starts[sb], nb), :])
            l = oe[:, HEAD_DIM:]
            if sink:
                l = l + jnp.exp2(sink_j - m)
            o = oe[:, :HEAD_DIM] / l
        o_ref[rows, j * HEAD_DIM:(j + 1) * HEAD_DIM] = o.astype(o_ref.dtype)

    items = [(sb, j) for sb in range(nsub) for j in range(KV_GROUP)]
    pending = scores(items[0])
    for prev, item in zip(items[:-1], items[1:]):
        nxt = scores(item)
        finish(prev, pending)
        pending = nxt
    finish(items[-1], pending)


def _attention(z1, cos, sin, q_gain, k_gain, sink_vec, *, mode, B, S, L, qcol, kcol, vcol):
    norm = q_gain is not None
    sink = sink_vec is not None
    lat0 = 0
    ctx0 = (B * S) // L
    if mode == 'ctx':
        tq, nt = L, 1
        q_map = lambda b, g, t: (ctx0 + b, qcol + g)
    else:
        tq = ATTN_ALL_ROWS if mode == 'all' else ATTN_BAND_ROWS
        nt = S // tq
        q_map = lambda b, g, t: (lat0 + b * nt + t, qcol + g)
    qw = KV_GROUP * HEAD_DIM
    args, in_specs = [], []
    if sink:
        args.append(sink_vec)
        in_specs.append(pl.BlockSpec(memory_space=pltpu.SMEM))
    args += [z1, z1, z1]
    in_specs += [
        pl.BlockSpec((tq, qw), q_map),
        pl.BlockSpec((L, HEAD_DIM), lambda b, g, t: (ctx0 + b, kcol + g)),
        pl.BlockSpec((L, HEAD_DIM), lambda b, g, t: (ctx0 + b, vcol + g)),
    ]
    if mode != 'ctx':
        args += [z1, z1, cos, sin]
        in_specs += [
            pl.BlockSpec((S, HEAD_DIM), lambda b, g, t: (b, kcol + g)),
            pl.BlockSpec((S, HEAD_DIM), lambda b, g, t: (b, vcol + g)),
            pl.BlockSpec((S, HEAD_DIM), lambda b, g, t: (0, 0)),
            pl.BlockSpec((S, HEAD_DIM), lambda b, g, t: (0, 0)),
        ]
    if norm:
        args += [q_gain.reshape(1, HEAD_DIM), k_gain.reshape(1, HEAD_DIM)]
        in_specs += [pl.BlockSpec((1, HEAD_DIM), lambda b, g, t: (0, 0))] * 2
    scratch = [pltpu.VMEM((KV_GROUP * tq, HEAD_DIM), BF16)]
    assert not (mode == 'band' and norm)
    if mode == 'all':
        scratch.append(pltpu.VMEM((L + S, HEAD_DIM), BF16))
        scratch.append(pltpu.VMEM((L + S, 2 * HEAD_DIM), BF16))
    elif mode == 'band':
        scratch.append(pltpu.VMEM((S, HEAD_DIM), BF16))
        scratch.append(pltpu.VMEM((L + S, 2 * HEAD_DIM), BF16))
    elif norm:
        scratch.append(pltpu.VMEM((L, HEAD_DIM), BF16))
    n_rows = B * (L if mode == 'ctx' else S)
    if mode == 'ctx':
        o_map = lambda b, g, t: (b, g)
    else:
        o_map = lambda b, g, t: (b * nt + t, g)
    return pl.pallas_call(
        functools.partial(_attn_kernel, mode=mode, norm=norm, sink=sink, tq=tq, S=S, L=L),
        grid=(B, ATTN_KV_HEADS, nt),
        in_specs=in_specs,
        out_specs=pl.BlockSpec((tq, qw), o_map),
        out_shape=jax.ShapeDtypeStruct((n_rows, BRANCH_WIDTH), BF16),
        scratch_shapes=scratch,
        compiler_params=_cparams(("arbitrary", "arbitrary", "arbitrary")),
        name="attn_%s%s" % (mode, "_sink" if sink else ""),
    )(*args)


CONV_ROWS = 256
HALO = 16


def _gdn_conv_kernel(sfirst_ref, slast_ref, prev_ref, cur_ref, next_ref, w_ref, o_ref):
    r = pl.program_id(0)
    part = pl.program_id(1)
    n = cur_ref.shape[0]
    row = lax.broadcasted_iota(jnp.int32, (n, HEAD_DIM), 0)
    is_first = sfirst_ref[r] == 1
    is_last = slast_ref[r] == 1
    for h in range(GDN_HEADS):
        cols = slice(h * HEAD_DIM, (h + 1) * HEAD_DIM)
        x = cur_ref[:, cols].astype(F32)
        prev_row = jnp.where(is_first, 0.0, prev_ref[:, cols].astype(F32)[HALO - 1:HALO, :])
        next_row = jnp.where(is_last, 0.0, next_ref[:, cols].astype(F32)[0:1, :])
        xm = jnp.where(row == 0, prev_row, pltpu.roll(x, 1, 0))
        xp = jnp.where(row == n - 1, next_row, pltpu.roll(x, n - 1, 0))
        y = _silu(w_ref[0:1, cols] * xm + w_ref[1:2, cols] * x + w_ref[2:3, cols] * xp)
        rs = lax.rsqrt(jnp.sum(y * y, axis=-1, keepdims=True) + RMS_EPS)
        factor = jnp.where(part == 0, rs * (HEAD_DIM ** -0.5), jnp.where(part == 1, rs, 1.0))
        o_ref[:, cols] = y * factor


def _gdn_conv(z1, conv_w, seq_first, seq_last):
    M = z1.shape[0]
    nblk = M // CONV_ROWS
    W = GDN_HEADS * HEAD_DIM
    c0 = 3072 // W
    per = CONV_ROWS // HALO
    nh = M // HALO
    return pl.pallas_call(
        _gdn_conv_kernel,
        grid_spec=pltpu.PrefetchScalarGridSpec(
            num_scalar_prefetch=2, grid=(nblk, 3),
            in_specs=[
                pl.BlockSpec((HALO, W), lambda r, c, a, b: (jnp.maximum(r * per - 1, 0), c0 + c)),
                pl.BlockSpec((CONV_ROWS, W), lambda r, c, a, b: (r, c0 + c)),
                pl.BlockSpec((HALO, W), lambda r, c, a, b: (jnp.minimum((r + 1) * per, nh - 1), c0 + c)),
                pl.BlockSpec((3, W), lambda r, c, a, b: (0, c)),
            ],
            out_specs=pl.BlockSpec((CONV_ROWS, W), lambda r, c, a, b: (r, c))),
        out_shape=jax.ShapeDtypeStruct((M, 3 * W), F32),
        compiler_params=_cparams(("arbitrary", "arbitrary")),
        name="gdn_conv",
    )(seq_first, seq_last, z1, z1, z1, conv_w)


def _chunk_tri(n, upper):
    r = lax.broadcasted_iota(jnp.int32, (n, n), 0)
    c = lax.broadcasted_iota(jnp.int32, (n, n), 1)
    same = (r // CHUNK) == (c // CHUNK)
    tri = (c >= r) if upper else (c <= r)
    return jnp.where(same & tri, 1.0, 0.0).astype(BF16)


def _gdn_gates_kernel(zg_ref, a_ref, dtb_ref, o_ref):
    zg = zg_ref[...]
    n = zg.shape[0]
    lane = lax.broadcasted_iota(jnp.int32, zg.shape, 1)
    beta = _sigmoid(zg)
    t = zg + dtb_ref[...]
    softplus = jnp.maximum(t, 0.0) + jnp.log(1.0 + jnp.exp(-jnp.abs(t)))
    g = jnp.where((lane >= 16) & (lane < 32), a_ref[...] * softplus, 0.0)
    gc_f = _dot_exact_lhs(_chunk_tri(n, False), g)
    gc_b = _dot_exact_lhs(_chunk_tri(n, True), g)
    o_ref[...] = jnp.where(lane < 16, beta, jnp.where(lane < 24, gc_f, gc_b))


def _gdn_gates(zg, a_row, dtb_row):
    M = zg.shape[0]
    tm = 256
    return pl.pallas_call(
        _gdn_gates_kernel,
        grid=(M // tm,),
        in_specs=[pl.BlockSpec((tm, LANES), lambda i: (i, 0)),
                  pl.BlockSpec((1, LANES), lambda i: (0, 0)),
                  pl.BlockSpec((1, LANES), lambda i: (0, 0))],
        out_specs=pl.BlockSpec((tm, LANES), lambda i: (i, 0)),
        out_shape=jax.ShapeDtypeStruct((M, LANES), F32),
        compiler_params=_cparams(("arbitrary",)),
        name="gdn_gates",
    )(zg, a_row, dtb_row)


_INV_BASE = 8


def _unit_tri_inverse(lms, r, c):
    C = lms[0].shape[0]
    eye = jnp.where(r == c, 1.0, 0.0)
    diag = (r // _INV_BASE) == (c // _INV_BASE)
    dms = [jnp.where(diag, lm, 0.0) for lm in lms]
    d16 = [dm.astype(BF16) for dm in dms]
    p16 = [_dot(d, d).astype(BF16) for d in d16]
    xs = [eye - dm for dm in dms]
    rrs = [_dot(jnp.concatenate([p, x.astype(BF16)], axis=0), p) for p, x in zip(p16, xs)]
    xs = [x + rr[C:] for x, rr in zip(xs, rrs)]
    ts = [x + _dot(x.astype(BF16), rr[:C].astype(BF16)) for x, rr in zip(xs, rrs)]
    s = _INV_BASE
    while s < C:
        join = ((r // (2 * s)) == (c // (2 * s))) & ((r // s) != (c // s))
        e16 = [jnp.where(join, lm, 0.0).astype(BF16) for lm in lms]
        t16 = [t.astype(BF16) for t in ts]
        ys = [_dot(e, t).astype(BF16) for e, t in zip(e16, t16)]
        ts = [t - _dot(tb, y) for t, tb, y in zip(ts, t16, ys)]
        s *= 2
    return ts


GDN_PREP_CHUNKS = 8


def _gdn_prep_kernel(q_ref, k_ref, v_ref, g_ref, rf_ref, rb_ref, wq_ref, u_ref, qkkd_ref, gl_ref):
    C = CHUNK
    h = pl.program_id(1)
    r = lax.broadcasted_iota(jnp.int32, (C, C), 0)
    c = lax.broadcasted_iota(jnp.int32, (C, C), 1)
    lane = lax.broadcasted_iota(jnp.int32, (C, LANES), 1)
    row_refs = (rf_ref, rb_ref)
    qs, ks, vs, kks, qks, gts = [], [], [], [], [], []
    for n in range(GDN_PREP_CHUNKS):
        rows = slice(n * C, (n + 1) * C)
        qs.append(q_ref[rows, :])
        ks.append(k_ref[rows, :])
        vs.append(v_ref[rows, :])
        gts.append(g_ref[rows, :])
        kb16 = ks[n].astype(BF16)
        kks.append(_dot_nt(kb16, kb16))
        qks.append(_dot_nt(qs[n].astype(BF16), kb16))

    def column(tile, ch):
        return jnp.sum(jnp.where(lane == ch, tile, 0.0), axis=-1, keepdims=True)

    units = [(n, d) for n in range(GDN_PREP_CHUNKS) for d in range(2)]
    lms, rhs1 = [], []
    for n, d in units:
        beta = column(gts[n], d * GDN_HEADS + h)
        gc = column(gts[n], 16 + d * GDN_HEADS + h)
        grow = row_refs[d][0, n][0:1, :]
        incl = (c <= r) if d == 0 else (c >= r)
        strict = (c < r) if d == 0 else (c > r)
        last = C - 1 if d == 0 else 0
        decay = jnp.where(incl, jnp.exp(jnp.where(incl, gc - grow, 0.0)), 0.0)
        lms.append(jnp.where(strict, beta * kks[n] * decay, 0.0))
        eg = jnp.exp(gc)
        g_last = gc[last:last + 1, :]
        rhs1.append(jnp.concatenate([ks[n] * (beta * eg), vs[n] * beta], axis=-1).astype(BF16))
        wq_ref[d, (2 * n + 1) * C:(2 * n + 2) * C, :] = (qs[n] * eg).astype(BF16)
        kd = ks[n] * jnp.exp(g_last - gc)
        base = n * (C + HEAD_DIM)
        qkkd_ref[d, 0, base:base + C, :] = (qks[n] * decay).astype(BF16)
        qkkd_ref[d, 0, base + C:base + C + HEAD_DIM, :] = jnp.transpose(kd).astype(BF16)
        gl_ref[d, 0, n * 8:(n + 1) * 8, :] = jnp.broadcast_to(jnp.exp(g_last), (8, HEAD_DIM))
    t16 = [t.astype(BF16) for t in _unit_tri_inverse(lms, r, c)]
    wa = [_dot(t, r1) for t, r1 in zip(t16, rhs1)]
    for (n, d), wu in zip(units, wa):
        wq_ref[d, 2 * n * C:(2 * n + 1) * C, :] = wu[:, :HEAD_DIM].astype(BF16)
        u_ref[d, n * C:(n + 1) * C, :] = wu[:, HEAD_DIM:]


def _gdn_prep(qkv, gates, rowb):
    M = qkv.shape[0]
    H = GDN_HEADS
    N = GDN_PREP_CHUNKS
    C = CHUNK
    nch = M // (C * N)
    rb_spec = lambda ch0: pl.BlockSpec((1, N, 8, C), lambda c, h: (ch0 + h, c, 0, 0))
    return pl.pallas_call(
        _gdn_prep_kernel,
        grid=(nch, H),
        in_specs=[
            pl.BlockSpec((N * C, HEAD_DIM), lambda c, h: (c, h)),
            pl.BlockSpec((N * C, HEAD_DIM), lambda c, h: (c, H + h)),
            pl.BlockSpec((N * C, HEAD_DIM), lambda c, h: (c, 2 * H + h)),
            pl.BlockSpec((N * C, LANES), lambda c, h: (c, 0)),
            rb_spec(0), rb_spec(H),
        ],
        out_specs=[
            pl.BlockSpec((2, N * 2 * C, HEAD_DIM), lambda c, h: (0, c, h)),
            pl.BlockSpec((2, N * C, HEAD_DIM), lambda c, h: (0, c, h)),
            pl.BlockSpec((2, 1, N * (C + HEAD_DIM), C), lambda c, h: (0, h, c, 0)),
            pl.BlockSpec((2, 1, N * 8, HEAD_DIM), lambda c, h: (0, h, c, 0)),
        ],
        name="gdn_prep",
        out_shape=[
            jax.ShapeDtypeStruct((2, 2 * M, H * HEAD_DIM), BF16),
            jax.ShapeDtypeStruct((2, M, H * HEAD_DIM), F32),
            jax.ShapeDtypeStruct((2, H, (M // C) * (C + HEAD_DIM), C), BF16),
            jax.ShapeDtypeStruct((2, H, (M // C) * 8, HEAD_DIM), F32),
        ],
        compiler_params=_cparams(("arbitrary", "arbitrary")),
    )(qkv, qkv, qkv, gates, rowb, rowb)


def _gdn_state_kernel(wqf_ref, uf_ref, qkf_ref, glf_ref, wqb_ref, ub_ref, qkb_ref, glb_ref,
                      of_ref, ob_ref, s_ref):
    C = CHUNK
    H = GDN_HEADS

    @pl.when(pl.program_id(1) == 0)
    def _():
        s_ref[...] = jnp.zeros(s_ref.shape, F32)

    dirs = ((wqf_ref, uf_ref, qkf_ref, glf_ref, of_ref), (wqb_ref, ub_ref, qkb_ref, glb_ref, ob_ref))
    units = [(d, h) for d in range(2) for h in range(H)]
    states = [s_ref[d * H + h] for d, h in units]
    for step in range(SCAN_CHUNKS):
        subs = [step if d == 0 else SCAN_CHUNKS - 1 - step for d, _ in units]
        m1s = []
        for (d, h), sub, st in zip(units, subs, states):
            cols = slice(h * HEAD_DIM, (h + 1) * HEAD_DIM)
            m1s.append(_dot(dirs[d][0][0, sub * 2 * C:(sub + 1) * 2 * C, cols], st.astype(BF16)))
        m2s = []
        for (d, h), sub, m1 in zip(units, subs, m1s):
            cols = slice(h * HEAD_DIM, (h + 1) * HEAD_DIM)
            v_new = dirs[d][1][0, sub * C:(sub + 1) * C, cols] - m1[0:C]
            m2s.append(_dot(dirs[d][2][0, h, sub * (C + HEAD_DIM):(sub + 1) * (C + HEAD_DIM), :],
                            v_new.astype(BF16)))
        new_states = []
        for (d, h), sub, st, m1, m2 in zip(units, subs, states, m1s, m2s):
            cols = slice(h * HEAD_DIM, (h + 1) * HEAD_DIM)
            dirs[d][4][sub * C:(sub + 1) * C, cols] = (m1[C:2 * C] + m2[0:C]).astype(BF16)
            new_states.append(st * dirs[d][3][0, h, sub * 8:sub * 8 + 1, :] + m2[C:])
        states = new_states
    for (d, h), st in zip(units, states):
        s_ref[d * H + h] = st


SCAN_CHUNKS = 4


def _seq_chunk_maps(B, S, L):
    C = CHUNK * SCAN_CHUNKS
    lc, sc = L // C, S // C
    ctx0 = (B * S) // C

    def fwd(b, c):
        return jnp.where(c < lc, ctx0 + b * lc + c, b * sc + (c - lc))

    def bwd(b, c):
        return jnp.where(c < lc, ctx0 + b * lc + (lc - 1 - c), b * sc + (sc - 1 - (c - lc)))

    return fwd, bwd, lc + sc


def _gdn_state(wq, u, qkkd, gl, *, B, S, L):
    C = CHUNK
    H = GDN_HEADS
    M = u.shape[1]
    fwd, bwd, nc = _seq_chunk_maps(B, S, L)

    N = SCAN_CHUNKS

    def specs(d, pos):
        return [
            pl.BlockSpec((1, N * 2 * C, H * HEAD_DIM), lambda b, c: (d, pos(b, c), 0)),
            pl.BlockSpec((1, N * C, H * HEAD_DIM), lambda b, c: (d, pos(b, c), 0)),
            pl.BlockSpec((1, H, N * (C + HEAD_DIM), C), lambda b, c: (d, 0, pos(b, c), 0)),
            pl.BlockSpec((1, H, N * 8, HEAD_DIM), lambda b, c: (d, 0, pos(b, c), 0)),
        ]

    return pl.pallas_call(
        _gdn_state_kernel,
        grid=(B, nc),
        in_specs=specs(0, fwd) + specs(1, bwd),
        out_specs=[pl.BlockSpec((N * C, H * HEAD_DIM), lambda b, c: (fwd(b, c), 0)),
                   pl.BlockSpec((N * C, H * HEAD_DIM), lambda b, c: (bwd(b, c), 0))],
        out_shape=[jax.ShapeDtypeStruct((M, H * HEAD_DIM), BF16)] * 2,
        scratch_shapes=[pltpu.VMEM((2 * H, HEAD_DIM, HEAD_DIM), F32)],
        compiler_params=_cparams(("arbitrary", "arbitrary")),
        name="gdn_state",
    )(wq, u, qkkd, gl, wq, u, qkkd, gl)


def _gla_kernel(qf_ref, kf_ref, vf_ref, zf_ref, qb_ref, kb_ref, vb_ref, zb_ref, wgk_ref, bgk_ref,
                of_ref, ob_ref, s_ref):
    C = CHUNK
    r = lax.broadcasted_iota(jnp.int32, (C, C), 0)
    c = lax.broadcasted_iota(jnp.int32, (C, C), 1)

    @pl.when(pl.program_id(1) == 0)
    def _():
        s_ref[...] = jnp.zeros(s_ref.shape, F32)

    dirs = ((qf_ref, kf_ref, vf_ref, zf_ref, of_ref), (qb_ref, kb_ref, vb_ref, zb_ref, ob_ref))
    incls = [(c <= r), (c >= r)]
    nrows = SCAN_CHUNKS * C
    bcums = []
    for d in range(2):
        pre = _dot_hi(dirs[d][3][...], wgk_ref[d]) + bgk_ref[d]
        gk = (jnp.minimum(pre, 0.0) - jnp.log(1.0 + jnp.exp(-jnp.abs(pre)))) * (1.0 / GLA_GATE_NORM)
        bcums.append(_dot_exact_lhs(_chunk_tri(nrows, d == 1), gk))
    units = [(d, h) for d in range(2) for h in range(GLA_HEADS)]
    states = [s_ref[d * GLA_HEADS + h] for d, h in units]
    for step in range(SCAN_CHUNKS):
        scores, inter, upd, dls = [], [], [], []
        for (d, h), st in zip(units, states):
            sub = step if d == 0 else SCAN_CHUNKS - 1 - step
            rows = slice(sub * C, (sub + 1) * C)
            kc = slice(h * GLA_DK, (h + 1) * GLA_DK)
            vc = slice(h * GLA_DV, (h + 1) * GLA_DV)
            mid = C // 2 - 1 if d == 1 else C // 2
            last = 0 if d == 1 else C - 1
            q = dirs[d][0][rows, kc].astype(F32) * (GLA_DK ** -0.5)
            k = dirs[d][1][rows, kc].astype(F32)
            b = bcums[d][sub * C:(sub + 1) * C, kc]
            b_mid = b[mid:mid + 1, :]
            b_last = b[last:last + 1, :]
            scores.append(_dot_nt((q * jnp.exp(b - b_mid)).astype(BF16), (k * jnp.exp(b_mid - b)).astype(BF16)))
            inter.append(_dot((q * jnp.exp(b)).astype(BF16), st.astype(BF16)))
            upd.append(_dot_tn((k * jnp.exp(b_last - b)).astype(BF16), dirs[d][2][rows, vc]))
            dls.append(jnp.transpose(jnp.broadcast_to(jnp.exp(b_last), (GLA_DK, GLA_DK))))
        new_states = []
        for (d, h), st, sc, oi, up, dl in zip(units, states, scores, inter, upd, dls):
            sub = step if d == 0 else SCAN_CHUNKS - 1 - step
            rows = slice(sub * C, (sub + 1) * C)
            vc = slice(h * GLA_DV, (h + 1) * GLA_DV)
            p = jnp.where(incls[d], sc, 0.0).astype(BF16)
            dirs[d][4][rows, vc] = (_dot(p, dirs[d][2][rows, vc]) + oi).astype(BF16)
            new_states.append(st * jnp.concatenate([dl, dl], axis=-1) + up)
        states = new_states
    for (d, h), st in zip(units, states):
        s_ref[d * GLA_HEADS + h] = st


def _gla(z2, zg, wgk_pad, bgk, *, B, S, L):
    C = CHUNK * SCAN_CHUNKS
    M = z2.shape[0]
    fwd, bwd, nc = _seq_chunk_maps(B, S, L)
    kw = GLA_HEADS * GLA_DK
    vw = GLA_HEADS * GLA_DV

    def specs(pos):
        return [
            pl.BlockSpec((C, kw), lambda b, c: (pos(b, c), 0)),
            pl.BlockSpec((C, kw), lambda b, c: (pos(b, c), 1)),
            pl.BlockSpec((C, vw), lambda b, c: (pos(b, c), 1)),
            pl.BlockSpec((C, LANES), lambda b, c: (pos(b, c), 0)),
        ]

    return pl.pallas_call(
        _gla_kernel,
        grid=(B, nc),
        in_specs=specs(fwd) + specs(bwd) + [
            pl.BlockSpec((2, LANES, kw), lambda b, c: (0, 0, 0)),
            pl.BlockSpec((2, 1, kw), lambda b, c: (0, 0, 0)),
        ],
        out_specs=[pl.BlockSpec((C, vw), lambda b, c: (fwd(b, c), 0)),
                   pl.BlockSpec((C, vw), lambda b, c: (bwd(b, c), 0))],
        out_shape=[jax.ShapeDtypeStruct((M, vw), BF16)] * 2,
        scratch_shapes=[pltpu.VMEM((2 * GLA_HEADS, GLA_DK, GLA_DV), F32)],
        compiler_params=_cparams(("arbitrary", "arbitrary")),
        name="gla",
    )(z2, z2, z2, zg, z2, z2, z2, zg, wgk_pad, bgk)


def _headnorm_kernel(of_ref, ob_ref, z_ref, g_ref, o_ref, *, hd):
    n = of_ref.shape[1] // hd
    for h in range(n):
        cols = slice(h * hd, (h + 1) * hd)
        o = of_ref[:, cols].astype(F32) + ob_ref[:, cols].astype(F32)
        y = o * lax.rsqrt(jnp.mean(o * o, axis=-1, keepdims=True) + RMS_EPS) * g_ref[...]
        o_ref[:, cols] = (y * _silu(z_ref[:, cols].astype(F32))).astype(o_ref.dtype)


def _headnorm(of, ob, z, zcol, gain, *, hd, tm=512):
    M, W = of.shape
    return pl.pallas_call(
        functools.partial(_headnorm_kernel, hd=hd),
        grid=(M // tm,),
        in_specs=[pl.BlockSpec((tm, W), lambda i: (i, 0)),
                  pl.BlockSpec((tm, W), lambda i: (i, 0)),
                  pl.BlockSpec((tm, W), lambda i: (i, zcol)),
                  pl.BlockSpec((1, hd), lambda i: (0, 0))],
        out_specs=pl.BlockSpec((tm, W), lambda i: (i, 0)),
        out_shape=jax.ShapeDtypeStruct((M, W), BF16),
        compiler_params=_cparams(("arbitrary",)),
        name="headnorm",
    )(of, ob, z, gain.reshape(1, hd))


def _final_kernel(rowid_ref, *refs, with_y, norm):
    del rowid_ref
    refs = list(refs)
    x = refs.pop(0)[...]
    if with_y:
        y0_ref, y1_ref, gt_ref = refs[:3]
        refs = refs[3:]
        x = x + gt_ref[0] * (y0_ref[...].astype(F32) + y1_ref[...].astype(F32))
    if norm:
        g_ref = refs.pop(0)
        x = x * lax.rsqrt(jnp.mean(x * x, axis=-1, keepdims=True) + RMS_EPS) * g_ref[...]
    refs[0][...] = x


def _final(X, ys, mod3, rowid, layer, k_gate, g_final, *, tm, grid_m):
    D = X.shape[1]
    row_spec = pl.BlockSpec((tm, D), lambda i, r: (i, 0))
    args, in_specs = [X], [row_spec]
    if ys is not None:
        args += [ys[0], ys[1], mod3]
        in_specs += [row_spec, row_spec,
                     pl.BlockSpec((1, 1, D), lambda i, r: (layer * 16 + r[i], 0, k_gate))]
    if g_final is not None:
        args.append(g_final.reshape(1, D))
        in_specs.append(pl.BlockSpec((1, D), lambda i, r: (0, 0)))
    return pl.pallas_call(
        functools.partial(_final_kernel, with_y=ys is not None, norm=g_final is not None),
        grid_spec=pltpu.PrefetchScalarGridSpec(
            num_scalar_prefetch=1, grid=(grid_m,), in_specs=in_specs, out_specs=row_spec),
        out_shape=jax.ShapeDtypeStruct((grid_m * tm, D), F32),
        compiler_params=_cparams(("arbitrary",)),
        name="final",
    )(rowid, *args)


def _rope_tables(S):
    rows = S // GRID_W
    row = jnp.repeat(jnp.arange(rows, dtype=F32), GRID_W)
    col = jnp.tile(jnp.arange(GRID_W, dtype=F32), rows)
    inv_freq = ROPE_THETA ** (-jnp.arange(ROPE_FREQS, dtype=F32) / ROPE_FREQS)
    ar = row[:, None] * inv_freq
    ac = col[:, None] * inv_freq
    cos = jnp.concatenate([jnp.cos(ar), jnp.cos(ar), jnp.cos(ac), jnp.cos(ac)], axis=-1)
    sin = jnp.concatenate([-jnp.sin(ar), jnp.sin(ar), -jnp.sin(ac), jnp.sin(ac)], axis=-1)
    return cos, sin


def _stream_rowid(B, S, L, tm, grid_m):
    i = jnp.arange(grid_m, dtype=jnp.int32) * tm
    return jnp.where(i < B * S, i // S, B).astype(jnp.int32)


def _moe_plan(logits, T, tm, n_tiles):
    top_val, top_idx = lax.top_k(logits, TOP_K)
    top_w = jax.nn.softmax(top_val, axis=-1)
    e_flat = top_idx.reshape(-1).astype(jnp.int32)
    w_flat = top_w.reshape(-1)
    n_assign = T * TOP_K
    order = jnp.argsort(e_flat, stable=True).astype(jnp.int32)
    inv = jnp.argsort(order).astype(jnp.int32)
    experts = jnp.arange(N_EXPERTS, dtype=jnp.int32)
    sizes = jnp.sum((e_flat[:, None] == experts[None, :]).astype(jnp.int32), axis=0)
    start = jnp.cumsum(sizes) - sizes
    padded = ((sizes + tm - 1) // tm) * tm
    pend = jnp.cumsum(padded)
    pstart = pend - padded
    dest = pstart[e_flat] + inv - start[e_flat]
    tile0 = jnp.arange(n_tiles, dtype=jnp.int32) * tm
    tile_e = jnp.minimum(jnp.searchsorted(pend, tile0, side='right'), N_EXPERTS - 1).astype(jnp.int32)
    valid = (tile0 < pend[-1]).astype(jnp.int32)
    row = jnp.arange(n_tiles * tm, dtype=jnp.int32)
    row_e = jnp.repeat(tile_e, tm)
    row_rank = row - pstart[row_e]
    row_ok = (row_rank < sizes[row_e]) & (jnp.repeat(valid, tm) == 1)
    row_a = order[jnp.clip(start[row_e] + row_rank, 0, n_assign - 1)]
    row_token = jnp.where(row_ok, row_a // TOP_K, row % T).astype(jnp.int32)
    row_w = jnp.where(row_ok, w_flat[row_a], 0.0)
    last_e = tile_e[jnp.maximum(pend[-1] // tm - 1, 0)]
    tile_e = jnp.where(valid == 1, tile_e, last_e)
    first = jnp.concatenate([jnp.ones((1,), jnp.int32), (tile_e[1:] != tile_e[:-1]).astype(jnp.int32)])
    return row_token, row_w, dest.reshape(T, TOP_K), tile_e, first, valid


def kernel(x, c, ctx, c_ctx, w_ada, b_ada, g_norm_mix, g_norm_ffn, w_in, a_q_gain, a_k_gain, b_sink,
           gdn_conv, gdn_a_log, gdn_dt_bias, gdn_norm_gain, gla_w_gk, gla_b_gk, gla_norm_gain,
           w_branch, w_merge, w_out, w1_dense, w3_dense, w2_dense, w_router, w1_moe, w3_moe, w2_moe,
           g_final):
    B, S, D = x.shape
    L = ctx.shape[1]
    depth = w_in.shape[0]
    n_lat = B * S
    M = n_lat + B * L
    TM = 512
    TM_BIG = 1024
    assert M % (CHUNK * GDN_PREP_CHUNKS) == 0 and L % (CHUNK * SCAN_CHUNKS) == 0
    assert D == D_MODEL and S % CONV_ROWS == 0 and L % CONV_ROWS == 0 and n_lat % TM == 0
    assert (B * L) % TM == 0 and S % TM == 0 and S >= 3 * Q_BLOCK and n_lat % L == 0 and B <= 15

    cond = jnp.zeros((16, D), F32).at[:B].set(c).at[B].set(c_ctx)
    cond2 = jnp.concatenate([cond] * depth, axis=0)
    (mod,) = _panel_call(
        _ep_ada, name="ada", grid_n=(6 * D) // 512, grid_m=depth,
        lhs=[(cond2, (16, D), lambda j, i, e, f, v, r: (i, 0))],
        weights=[(w_ada, (1, D, 512), lambda j, i, e, f, v, r: (e[i], 0, j))],
        extras=[(b_ada.reshape(depth, 1, 6 * D), (1, 1, 512), lambda j, i, e, f, v, r: (e[i], 0, j))],
        outs=[(jax.ShapeDtypeStruct((16 * depth, 6 * D), F32), (16, 512), lambda j, i, e, f, v, r: (i, j))],
        eid=jnp.arange(depth, dtype=jnp.int32), first=jnp.ones((depth,), jnp.int32))
    mod3 = mod.reshape(16 * depth, 1, 6 * D)

    cos, sin = _rope_tables(S)
    w_in_t = jnp.swapaxes(w_in, 1, 2)
    X = jnp.concatenate([x.reshape(n_lat, D), ctx.reshape(B * L, D)], axis=0)
    gm_all = M // TM
    gm_lat = n_lat // TM
    rowid = _stream_rowid(B, S, L, TM, gm_all)

    blk = jnp.arange(M // CONV_ROWS, dtype=jnp.int32) * CONV_ROWS
    seq_len = jnp.where(blk < n_lat, S, L)
    seq_off = jnp.where(blk < n_lat, blk, blk - n_lat)
    seq_first = (seq_off % seq_len == 0).astype(jnp.int32)
    seq_last = ((seq_off + CONV_ROWS) % seq_len == 0).astype(jnp.int32)

    out = None
    for layer in range(depth):
        last = layer == depth - 1
        gm = gm_lat if last else gm_all
        h = _modulate(X, g_norm_mix, mod3, rowid, layer, 0, 1, tm=TM, grid_m=M // TM)
        tm_z = TM_BIG if M % TM_BIG == 0 else TM
        z1 = _matmul_stream(h, w_in_t, layer, 0, Z1_WIDTH, name="proj_z1", tm=tm_z, tn=1024, out_dtype=BF16)
        w_d = lax.slice(w_in_t, (layer, COL_D_Q, 0), (layer + 1, COL_D_GATE, D)).reshape(Z2_WIDTH, D)
        z2 = _matmul_stream(h, w_d, layer, 0, Z2_WIDTH, name="proj_z2", tm=tm_z, tn=1024, out_dtype=BF16)
        w_g = jnp.concatenate([
            lax.slice(w_in_t, (layer, COL_C_BETA, 0), (layer + 1, COL_D_Q, D)).reshape(COL_D_Q - COL_C_BETA, D),
            lax.slice(w_in_t, (layer, COL_D_GATE, 0), (layer + 1, PROJ_WIDTH, D)).reshape(PROJ_WIDTH - COL_D_GATE, D),
            jnp.zeros((LANES - 64, D), F32)], axis=0)
        zg = _matmul_stream(h, w_g, layer, 0, LANES, name="proj_gates", tm=TM, tn=LANES, out_dtype=F32)

        oa = _attention(z1, cos, sin, a_q_gain[layer], a_k_gain[layer], None, mode='all',
                        B=B, S=S, L=L, qcol=0, kcol=8, vcol=10)
        ob = _attention(z1, cos, sin, None, None, b_sink[layer], mode='band',
                        B=B, S=S, L=L, qcol=3, kcol=20, vcol=22)
        if not last:
            oa_c = _attention(z1, cos, sin, a_q_gain[layer], a_k_gain[layer], None, mode='ctx',
                              B=B, S=S, L=L, qcol=0, kcol=8, vcol=10)
            ob_c = _attention(z1, cos, sin, None, None, b_sink[layer], mode='ctx',
                              B=B, S=S, L=L, qcol=3, kcol=20, vcol=22)
            oa = jnp.concatenate([oa, oa_c], axis=0)
            ob = jnp.concatenate([ob, ob_c], axis=0)

        qkv = _gdn_conv(z1, gdn_conv[layer], seq_first, seq_last)
        a_row = jnp.zeros((1, LANES), F32).at[0, 16:32].set(-jnp.exp(gdn_a_log[layer]).reshape(-1))
        dtb_row = jnp.zeros((1, LANES), F32).at[0, 16:32].set(gdn_dt_bias[layer].reshape(-1))
        gates = _gdn_gates(zg, a_row, dtb_row)
        gt = gates[:, 16:32].T
        rowb = jnp.broadcast_to(gt.reshape(16, M // CHUNK, 1, CHUNK), (16, M // CHUNK, 8, CHUNK))
        wq, u, qkkd, gl = _gdn_prep(qkv, gates, rowb)
        oc_f, oc_b = _gdn_state(wq, u, qkkd, gl, B=B, S=S, L=L)
        oc = _headnorm(oc_f, oc_b, z1, 6144 // BRANCH_WIDTH, gdn_norm_gain[layer], hd=HEAD_DIM)

        wgk_pad = jnp.zeros((2, LANES, GLA_HEADS * GLA_DK), F32)
        wgk_pad = wgk_pad.at[0, 32:48].set(gla_w_gk[layer, 0]).at[1, 48:64].set(gla_w_gk[layer, 1])
        od_f, od_b = _gla(z2, zg, wgk_pad, gla_b_gk[layer].reshape(2, 1, -1), B=B, S=S, L=L)
        od = _headnorm(od_f, od_b, z2, 2048 // BRANCH_WIDTH, gla_norm_gain[layer], hd=GLA_DV)

        tn_m = 256
        branches = [oa, ob, oc, od]
        lhs = [(h, (TM, D), lambda j, i, e, f, v, r: (i, 0))]
        lhs += [(o, (TM, BRANCH_WIDTH), lambda j, i, e, f, v, r: (i, 0)) for o in branches]
        wts = [(w_merge, (1, D, tn_m), lambda j, i, e, f, v, r, n=n: (layer, 0, n * (D // tn_m) + j))
               for n in range(N_BRANCH)]
        wts += [(w_branch, (1, 1, BRANCH_WIDTH, tn_m), lambda j, i, e, f, v, r, n=n: (layer, n, 0, j))
                for n in range(N_BRANCH)]
        (merged,) = _panel_call(
            _ep_merge, name="merge", grid_n=D // tn_m, grid_m=gm, lhs=lhs, weights=wts, extras=[],
            outs=[(jax.ShapeDtypeStruct((gm * TM, D), BF16), (TM, tn_m), lambda j, i, e, f, v, r: (i, j))])

        def resid(name, xin, w, wmap, kdim, k_gate, tn):
            (res,) = _panel_call(
                _ep_resid, name=name, grid_n=D // tn, grid_m=gm,
                lhs=[(xin, (TM, kdim), lambda j, i, e, f, v, r: (i, 0))],
                weights=[(w, (1, kdim, tn), wmap)],
                extras=[(X, (TM, tn), lambda j, i, e, f, v, r: (i, j)),
                        (mod3, (1, 1, tn), lambda j, i, e, f, v, r: (layer * 16 + r[i], 0, k_gate * (D // tn) + j))],
                outs=[(jax.ShapeDtypeStruct((gm * TM, D), F32), (TM, tn), lambda j, i, e, f, v, r: (i, j))],
                rowid=rowid[:gm])
            return res

        X = resid("out_proj", merged, w_out, lambda j, i, e, f, v, r: (layer, 0, j), D, 2, 1024)

        if layer % 2 == 0:
            li = layer // 2
            h2 = _modulate(X, g_norm_ffn, mod3, rowid, layer, 3, 4, tm=TM, grid_m=gm)
            d_ff = w1_dense.shape[2]
            tm_u = TM_BIG if (gm * TM) % TM_BIG == 0 else TM
            (uu,) = _panel_call(
                _ep_swiglu, name="dense_up", grid_n=d_ff // 512, grid_m=(gm * TM) // tm_u,
                lhs=[(h2, (tm_u, D), lambda j, i, e, f, v, r: (i, 0))],
                weights=[(w1_dense, (1, D, 512), lambda j, i, e, f, v, r: (li, 0, j)),
                         (w3_dense, (1, D, 512), lambda j, i, e, f, v, r: (li, 0, j))],
                extras=[],
                outs=[(jax.ShapeDtypeStruct((gm * TM, d_ff), BF16), (tm_u, 512), lambda j, i, e, f, v, r: (i, j))])
            X = resid("dense_down", uu, w2_dense, lambda j, i, e, f, v, r: (li, 0, j), d_ff, 5, 512)
            if last:
                out = _final(X, None, mod3, rowid, layer, 5, g_final, tm=TM, grid_m=n_lat // TM)
        else:
            li = layer // 2
            T = gm * TM
            wr_pad = jnp.zeros((D, LANES), F32).at[:, :N_EXPERTS].set(w_router[li])
            h2, logits = _modulate(X, g_norm_ffn, mod3, rowid, layer, 3, 4, tm=TM, grid_m=gm, w_router=wr_pad)
            tm_e = 512
            n_tiles = (T * TOP_K) // tm_e + N_EXPERTS
            row_token, row_w, dest, tile_e, first, valid = _moe_plan(logits[:, :N_EXPERTS], T, tm_e, n_tiles)
            xg = h2.at[row_token].get(mode="promise_in_bounds")
            d_ffe = w1_moe.shape[3]
            (uu,) = _panel_call(
                _ep_swiglu, name="moe_up", grid_n=d_ffe // 1024, grid_m=n_tiles,
                lhs=[(xg, (tm_e, D), lambda j, i, e, f, v, r: (i, 0))],
                weights=[(w1_moe, (1, 1, D, 1024), lambda j, i, e, f, v, r: (li, e[i], 0, j)),
                         (w3_moe, (1, 1, D, 1024), lambda j, i, e, f, v, r: (li, e[i], 0, j))],
                extras=[],
                outs=[(jax.ShapeDtypeStruct((n_tiles * tm_e, d_ffe), BF16), (tm_e, 1024),
                       lambda j, i, e, f, v, r: (i, j))],
                eid=tile_e, first=first, valid=valid)
            tm_d = 512
            tn_d = 512
            rep = tm_e // tm_d
            first_d = jnp.repeat(first, rep) * (jnp.arange(n_tiles * rep, dtype=jnp.int32) % rep == 0)
            (yo,) = _panel_call(
                _ep_rowscale, name="moe_down", grid_n=D // tn_d, grid_m=n_tiles * rep,
                lhs=[(uu, (tm_d, d_ffe), lambda j, i, e, f, v, r: (i, 0))],
                weights=[(w2_moe, (1, 1, d_ffe, tn_d), lambda j, i, e, f, v, r: (li, e[i], 0, j))],
                extras=[(row_w.reshape(-1, 1), (tm_d, 1), lambda j, i, e, f, v, r: (i, 0))],
                outs=[(jax.ShapeDtypeStruct((n_tiles * tm_e, D), BF16), (tm_d, tn_d),
                       lambda j, i, e, f, v, r: (i, j))],
                eid=jnp.repeat(tile_e, rep), first=first_d.astype(jnp.int32), valid=jnp.repeat(valid, rep))
            ys = (yo.at[dest[:, 0]].get(mode="promise_in_bounds"),
                  yo.at[dest[:, 1]].get(mode="promise_in_bounds"))
            res = _final(X, ys, mod3, rowid, layer, 5, g_final if last else None, tm=TM, grid_m=gm)
            if last:
                out = res
            else:
                X = res
    return out.reshape(B, S, D)
```

```python
import functools

import jax
import jax.numpy as jnp
from jax import lax
from jax.experimental import pallas as pl
from jax.experimental.pallas import tpu as pltpu

F32 = jnp.float32
BF16 = jnp.bfloat16

D_MODEL = 2048
HEAD_DIM = 128
GRID_W = 64
ROPE_THETA = 10000.0
ROPE_FREQS = HEAD_DIM // 4
RMS_EPS = 1e-6
BRANCH_WIDTH = D_MODEL // 2
KV_GROUP = 4
ATTN_KV_HEADS = 2
WINDOW = 128
Q_BLOCK = 128
GDN_HEADS = 8
GLA_HEADS = 4
GLA_DK = 128
GLA_DV = 256
GLA_RANK = 16
GLA_GATE_NORM = 16.0
CHUNK = 64
N_EXPERTS = 8
TOP_K = 2
N_BRANCH = 4
NEG_BIG = -1e30
LOG2E = 1.4426950408889634

COL_C_BETA = 7168
COL_D_Q = 7200
COL_D_GATE = 10272
PROJ_WIDTH = 10304
Z1_WIDTH = 7168
Z2_WIDTH = 3072

VMEM_LIMIT = 56 * 1024 * 1024
LANES = 128


def _cparams(sem):
    return pltpu.CompilerParams(dimension_semantics=sem, vmem_limit_bytes=VMEM_LIMIT)


def _silu(x):
    return x * (1.0 / (1.0 + jnp.exp(-x)))


def _sigmoid(x):
    return 1.0 / (1.0 + jnp.exp(-x))


def _split3(x):
    x1 = x.astype(BF16)
    r1 = x - x1.astype(F32)
    x2 = r1.astype(BF16)
    x3 = (r1 - x2.astype(F32)).astype(BF16)
    return x1, x2, x3


def _dot(a, b):
    return jnp.dot(a, b, preferred_element_type=F32)


def _dot_nt(a, b):
    return lax.dot_general(a, b, (((1,), (1,)), ((), ())), preferred_element_type=F32)


def _dot_tn(a, b):
    return lax.dot_general(a, b, (((0,), (0,)), ((), ())), preferred_element_type=F32)


def _dot_exact_lhs(tri_bf16, x):
    x1, x2, x3 = _split3(x)
    return _dot(tri_bf16, x1) + _dot(tri_bf16, x2) + _dot(tri_bf16, x3)


def _dot_hi(a, b):
    a1 = a.astype(BF16)
    a2 = (a - a1.astype(F32)).astype(BF16)
    b1 = b.astype(BF16)
    b2 = (b - b1.astype(F32)).astype(BF16)
    return _dot(a1, b1) + _dot(a1, b2) + _dot(a2, b1)


_CAST_ROWS = 256


def _panel_kernel(eid_ref, first_ref, valid_ref, rowid_ref, *refs, n_lhs, n_w, n_ex, n_out,
                  w_rows, epilogue):
    del eid_ref, rowid_ref
    lhs = refs[:n_lhs]
    ws = refs[n_lhs:n_lhs + n_w]
    exs = refs[n_lhs + n_w:n_lhs + n_w + n_ex]
    outs = refs[n_lhs + n_w + n_ex:n_lhs + n_w + n_ex + n_out]
    wbs = refs[n_lhs + n_w + n_ex + n_out:]
    i = pl.program_id(1)

    @pl.when(first_ref[i] == 1)
    def _():
        for w_ref, wb_ref, rows in zip(ws, wbs, w_rows):
            lead = (0,) * (len(w_ref.shape) - 2)
            step = min(_CAST_ROWS, rows)
            assert rows % step == 0

            def cast(r, carry, w_ref=w_ref, wb_ref=wb_ref, lead=lead, step=step):
                rr = pl.multiple_of(r * step, step)
                wb_ref[pl.ds(rr, step), :] = w_ref[lead + (pl.ds(rr, step), slice(None))].astype(BF16)
                return carry

            lax.fori_loop(0, rows // step, cast, 0)

    @pl.when(valid_ref[i] == 1)
    def _():
        epilogue(lhs, wbs, exs, outs)

    @pl.when(valid_ref[i] == 0)
    def _():
        for o in outs:
            o[...] = jnp.zeros(o.shape, o.dtype)


def _panel_call(epilogue, *, name, grid_n, grid_m, lhs, weights, extras, outs, eid=None, first=None,
                valid=None, rowid=None):
    if eid is None:
        eid = jnp.zeros((grid_m,), jnp.int32)
    if first is None:
        first = jnp.zeros((grid_m,), jnp.int32).at[0].set(1)
    if valid is None:
        valid = jnp.ones((grid_m,), jnp.int32)
    if rowid is None:
        rowid = jnp.zeros((grid_m,), jnp.int32)
    in_arrays, in_specs = [], []
    for arr, blk, imap in list(lhs) + list(weights) + list(extras):
        in_arrays.append(arr)
        in_specs.append(pl.BlockSpec(blk, imap))
    out_shapes = [o[0] for o in outs]
    out_specs = [pl.BlockSpec(o[1], o[2]) for o in outs]
    w_rows = [blk[-2] for _, blk, _ in weights]
    scratch = [pltpu.VMEM((blk[-2], blk[-1]), BF16) for _, blk, _ in weights]
    kern = functools.partial(_panel_kernel, n_lhs=len(lhs), n_w=len(weights), n_ex=len(extras),
                             n_out=len(outs), w_rows=w_rows, epilogue=epilogue)
    res = pl.pallas_call(
        kern,
        grid_spec=pltpu.PrefetchScalarGridSpec(
            num_scalar_prefetch=4, grid=(grid_n, grid_m),
            in_specs=in_specs, out_specs=out_specs, scratch_shapes=scratch),
        out_shape=out_shapes,
        compiler_params=_cparams(("arbitrary", "arbitrary")),
        name=name,
    )(eid, first, valid, rowid, *in_arrays)
    return res


def _ep_ada(lhs, wbs, exs, outs):
    x = _silu(lhs[0][...]).astype(BF16)
    outs[0][...] = _dot(x, wbs[0][...]) + exs[0][0]


def _ep_swiglu(lhs, wbs, exs, outs):
    x = lhs[0][...]
    a1 = _dot(x, wbs[0][...])
    a3 = _dot(x, wbs[1][...])
    outs[0][...] = (_silu(a1) * a3).astype(outs[0].dtype)


def _ep_resid(lhs, wbs, exs, outs):
    outs[0][...] = exs[0][...] + exs[1][0] * _dot(lhs[0][...], wbs[0][...])


def _ep_rowscale(lhs, wbs, exs, outs):
    outs[0][...] = (exs[0][...] * _dot(lhs[0][...], wbs[0][...])).astype(outs[0].dtype)


def _ep_merge(lhs, wbs, exs, outs):
    h = lhs[0][...]
    acc = None
    for n in range(N_BRANCH):
        gate = _sigmoid(_dot(h, wbs[n][...]))
        y = _dot(lhs[1 + n][...], wbs[N_BRANCH + n][...])
        acc = gate * y if acc is None else acc + gate * y
    outs[0][...] = acc.astype(outs[0].dtype)


def _ep_plain_nt(lhs, wbs, exs, outs):
    outs[0][...] = _dot_nt(lhs[0][...], wbs[0][...]).astype(outs[0].dtype)


def _matmul_stream(x, wt, layer, row0, n_cols, *, name, tm, tn, out_dtype, grid_m=None):
    M, K = x.shape
    grid_m = M // tm if grid_m is None else grid_m
    jb = row0 // tn
    if wt.ndim == 3:
        wspec = (wt, (1, tn, K), lambda j, i, e, f, v, r: (layer, jb + j, 0))
    else:
        wspec = (wt, (tn, K), lambda j, i, e, f, v, r: (jb + j, 0))
    (out,) = _panel_call(
        _ep_plain_nt, name=name, grid_n=n_cols // tn, grid_m=grid_m,
        lhs=[(x, (tm, K), lambda j, i, e, f, v, r: (i, 0))],
        weights=[wspec], extras=[],
        outs=[(jax.ShapeDtypeStruct((grid_m * tm, n_cols), out_dtype), (tm, tn),
               lambda j, i, e, f, v, r: (i, j))])
    return out


def _modulate_kernel(rowid_ref, x_ref, g_ref, sh_ref, sc_ref, *rest, with_router):
    del rowid_ref
    x = x_ref[...]
    y = x * lax.rsqrt(jnp.mean(x * x, axis=-1, keepdims=True) + RMS_EPS)
    h = (y * g_ref[0]) * (1.0 + sc_ref[0]) + sh_ref[0]
    if with_router:
        wr_ref, h_ref, lg_ref = rest
        h_ref[...] = h.astype(BF16)
        lg_ref[...] = _dot_hi(h, wr_ref[...])
    else:
        (h_ref,) = rest
        h_ref[...] = h.astype(BF16)


def _modulate(X, gain, mod3, rowid, layer, k_shift, k_scale, *, tm, grid_m, w_router=None):
    D = X.shape[1]
    with_router = w_router is not None
    in_specs = [
        pl.BlockSpec((tm, D), lambda i, r: (i, 0)),
        pl.BlockSpec((1, 1, D), lambda i, r: (layer, 0, 0)),
        pl.BlockSpec((1, 1, D), lambda i, r: (layer * 16 + r[i], 0, k_shift)),
        pl.BlockSpec((1, 1, D), lambda i, r: (layer * 16 + r[i], 0, k_scale)),
    ]
    args = [X, gain.reshape(gain.shape[0], 1, D), mod3, mod3]
    out_shape = [jax.ShapeDtypeStruct((grid_m * tm, D), BF16)]
    out_specs = [pl.BlockSpec((tm, D), lambda i, r: (i, 0))]
    if with_router:
        in_specs.append(pl.BlockSpec((D, LANES), lambda i, r: (0, 0)))
        args.append(w_router)
        out_shape.append(jax.ShapeDtypeStruct((grid_m * tm, LANES), F32))
        out_specs.append(pl.BlockSpec((tm, LANES), lambda i, r: (i, 0)))
    res = pl.pallas_call(
        functools.partial(_modulate_kernel, with_router=with_router),
        grid_spec=pltpu.PrefetchScalarGridSpec(
            num_scalar_prefetch=1, grid=(grid_m,), in_specs=in_specs, out_specs=out_specs),
        out_shape=out_shape,
        compiler_params=_cparams(("arbitrary",)),
        name="modulate_router" if with_router else "modulate",
    )(rowid, *args)
    return res if with_router else res[0]


def _swap_halves(x):
    lane = lax.broadcasted_iota(jnp.int32, x.shape, x.ndim - 1)
    return jnp.where((lane % 64) < 32, pltpu.roll(x, 96, x.ndim - 1), pltpu.roll(x, 32, x.ndim - 1))


def _rope(x, cos, sin_signed):
    return x * cos + _swap_halves(x) * sin_signed


def _head_rms(x, gain_row):
    return x * lax.rsqrt(jnp.mean(x * x, axis=-1, keepdims=True) + RMS_EPS) * gain_row


ATTN_ALL_ROWS = 256
ATTN_BAND_ROWS = 2 * Q_BLOCK


def _lane_tile_max(s):
    m = s[:, 0:LANES]
    for k in range(1, s.shape[1] // LANES):
        m = jnp.maximum(m, s[:, k * LANES:(k + 1) * LANES])
    return m


def _attn_kernel(*refs, mode, norm, sink, tq, S, L):
    refs = list(refs)
    sink_ref = refs.pop(0) if sink else None
    q_ref, kc_ref, vc_ref = refs[:3]
    refs = refs[3:]
    if mode != 'ctx':
        kl_ref, vl_ref, cos_ref, sin_ref = refs[:4]
        refs = refs[4:]
    if norm:
        qg_ref, kg_ref = refs[:2]
        refs = refs[2:]
    o_ref, qs_ref = refs[:2]
    kp_ref = refs[2] if len(refs) > 2 else None
    ve_ref = refs[3] if len(refs) > 3 else None
    g = pl.program_id(1)
    t = pl.program_id(2)
    rows = 256

    if kp_ref is not None:
        @pl.when(t == 0)
        def _():
            if mode != 'band':
                for r0 in range(0, L, rows):
                    kc = kc_ref[r0:r0 + rows, :].astype(F32)
                    if norm:
                        kc = _head_rms(kc, kg_ref[...])
                    kp_ref[r0:r0 + rows, :] = kc.astype(BF16)
            if mode != 'ctx':
                base = L if mode == 'all' else 0
                for r0 in range(0, S, rows):
                    kl = kl_ref[r0:r0 + rows, :].astype(F32)
                    if norm:
                        kl = _head_rms(kl, kg_ref[...])
                    kl = _rope(kl, cos_ref[r0:r0 + rows, :], sin_ref[r0:r0 + rows, :])
                    kp_ref[base + r0:base + r0 + rows, :] = kl.astype(BF16)
                ve_ref[0:L, 0:HEAD_DIM] = vc_ref[...]
                ve_ref[L:L + S, 0:HEAD_DIM] = vl_ref[...]
                ve_ref[:, HEAD_DIM:2 * HEAD_DIM] = jnp.ones((L + S, HEAD_DIM), BF16)

    scale = (HEAD_DIM ** -0.5) * LOG2E
    if mode != 'ctx':
        t0 = pl.multiple_of(t * tq, tq)
        cos_q = cos_ref[pl.ds(t0, tq), :]
        sin_q = sin_ref[pl.ds(t0, tq), :]
    rq = Q_BLOCK if mode == 'band' else tq
    nsub = tq // rq
    if mode == 'band':
        nb = 3 * Q_BLOCK
        starts, in_bands = [], []
        for sb in range(nsub):
            blk0 = t * tq + sb * rq
            starts.append(pl.multiple_of(jnp.clip(blk0 - WINDOW, 0, S - nb), Q_BLOCK))
            qpos = blk0 + lax.broadcasted_iota(jnp.int32, (rq, nb), 0)
            kpos = starts[sb] + lax.broadcasted_iota(jnp.int32, (rq, nb), 1)
            in_bands.append(jnp.abs(qpos - kpos) <= WINDOW)

    def scores(item):
        sb, j = item
        rows = slice(sb * rq, (sb + 1) * rq)
        qj = q_ref[rows, j * HEAD_DIM:(j + 1) * HEAD_DIM].astype(F32)
        if norm:
            qj = _head_rms(qj, qg_ref[...])
        if mode != 'ctx':
            qj = _rope(qj, cos_q[rows], sin_q[rows])
        base = (j * nsub + sb) * rq
        qs_ref[base:base + rq, :] = (qj * scale).astype(BF16)
        qj = qs_ref[base:base + rq, :]
        if mode == 'all':
            half = (L + S) // 2
            return _dot_nt(qj, kp_ref[0:half, :]), _dot_nt(qj, kp_ref[half:L + S, :])
        if mode == 'band':
            s_loc = _dot_nt(qj, kp_ref[pl.ds(starts[sb], nb), :])
            return _dot_nt(qj, kc_ref[...]), jnp.where(in_bands[sb], s_loc, NEG_BIG)
        return (_dot_nt(qj, kp_ref[...] if norm else kc_ref[...]),)

    def finish(item, sc):
        sb, j = item
        rows = slice(sb * rq, (sb + 1) * rq)
        sink_j = sink_ref[g * KV_GROUP + j] * LOG2E if sink else None
        if mode == 'ctx':
            (s,) = sc
            m = jnp.max(s, axis=-1, keepdims=True)
            if sink:
                m = jnp.maximum(m, sink_j)
            p = jnp.exp2(s - m)
            l = jnp.sum(p, axis=-1, keepdims=True)
            if sink:
                l = l + jnp.exp2(sink_j - m)
            o = _dot(p.astype(BF16), vc_ref[...]) / l
        else:
            s_a, s_b = sc
            m = jnp.max(jnp.maximum(_lane_tile_max(s_a), _lane_tile_max(s_b)), axis=-1, keepdims=True)
            if sink:
                m = jnp.maximum(m, sink_j)
            p_a = jnp.exp2((s_a - m).astype(BF16))
            p_b = jnp.exp2((s_b - m).astype(BF16))
            if mode == 'all':
                half = (L + S) // 2
                oe = _dot(p_a, ve_ref[0:half, :]) + _dot(p_b, ve_ref[half:L + S, :])
            else:
                oe = _dot(p_a, ve_ref[0:L, :]) + _dot(p_b, ve_ref[pl.ds(L + starts[sb], nb), :])
            l = oe[:, HEAD_DIM:]
            if sink:
                l = l + jnp.exp2(sink_j - m)
            o = oe[:, :HEAD_DIM] / l
        o_ref[rows, j * HEAD_DIM:(j + 1) * HEAD_DIM] = o.astype(o_ref.dtype)

    items = [(sb, j) for sb in range(nsub) for j in range(KV_GROUP)]
    pending = scores(items[0])
    for prev, item in zip(items[:-1], items[1:]):
        nxt = scores(item)
        finish(prev, pending)
        pending = nxt
    finish(items[-1], pending)


def _attention(z1, cos, sin, q_gain, k_gain, sink_vec, *, mode, B, S, L, qcol, kcol, vcol):
    norm = q_gain is not None
    sink = sink_vec is not None
    lat0 = 0
    ctx0 = (B * S) // L
    if mode == 'ctx':
        tq, nt = L, 1
        q_map = lambda b, g, t: (ctx0 + b, qcol + g)
    else:
        tq = ATTN_ALL_ROWS if mode == 'all' else ATTN_BAND_ROWS
        nt = S // tq
        q_map = lambda b, g, t: (lat0 + b * nt + t, qcol + g)
    qw = KV_GROUP * HEAD_DIM
    args, in_specs = [], []
    if sink:
        args.append(sink_vec)
        in_specs.append(pl.BlockSpec(memory_space=pltpu.SMEM))
    args += [z1, z1, z1]
    in_specs += [
        pl.BlockSpec((tq, qw), q_map),
        pl.BlockSpec((L, HEAD_DIM), lambda b, g, t: (ctx0 + b, kcol + g)),
        pl.BlockSpec((L, HEAD_DIM), lambda b, g, t: (ctx0 + b, vcol + g)),
    ]
    if mode != 'ctx':
        args += [z1, z1, cos, sin]
        in_specs += [
            pl.BlockSpec((S, HEAD_DIM), lambda b, g, t: (b, kcol + g)),
            pl.BlockSpec((S, HEAD_DIM), lambda b, g, t: (b, vcol + g)),
            pl.BlockSpec((S, HEAD_DIM), lambda b, g, t: (0, 0)),
            pl.BlockSpec((S, HEAD_DIM), lambda b, g, t: (0, 0)),
        ]
    if norm:
        args += [q_gain.reshape(1, HEAD_DIM), k_gain.reshape(1, HEAD_DIM)]
        in_specs += [pl.BlockSpec((1, HEAD_DIM), lambda b, g, t: (0, 0))] * 2
    scratch = [pltpu.VMEM((KV_GROUP * tq, HEAD_DIM), BF16)]
    assert not (mode == 'band' and norm)
    if mode == 'all':
        scratch.append(pltpu.VMEM((L + S, HEAD_DIM), BF16))
        scratch.append(pltpu.VMEM((L + S, 2 * HEAD_DIM), BF16))
    elif mode == 'band':
        scratch.append(pltpu.VMEM((S, HEAD_DIM), BF16))
        scratch.append(pltpu.VMEM((L + S, 2 * HEAD_DIM), BF16))
    elif norm:
        scratch.append(pltpu.VMEM((L, HEAD_DIM), BF16))
    n_rows = B * (L if mode == 'ctx' else S)
    if mode == 'ctx':
        o_map = lambda b, g, t: (b, g)
    else:
        o_map = lambda b, g, t: (b * nt + t, g)
    return pl.pallas_call(
        functools.partial(_attn_kernel, mode=mode, norm=norm, sink=sink, tq=tq, S=S, L=L),
        grid=(B, ATTN_KV_HEADS, nt),
        in_specs=in_specs,
        out_specs=pl.BlockSpec((tq, qw), o_map),
        out_shape=jax.ShapeDtypeStruct((n_rows, BRANCH_WIDTH), BF16),
        scratch_shapes=scratch,
        compiler_params=_cparams(("arbitrary", "arbitrary", "arbitrary")),
        name="attn_%s%s" % (mode, "_sink" if sink else ""),
    )(*args)


CONV_ROWS = 256
HALO = 16


def _gdn_conv_kernel(sfirst_ref, slast_ref, prev_ref, cur_ref, next_ref, w_ref, o_ref):
    r = pl.program_id(0)
    part = pl.program_id(1)
    n = cur_ref.shape[0]
    row = lax.broadcasted_iota(jnp.int32, (n, HEAD_DIM), 0)
    is_first = sfirst_ref[r] == 1
    is_last = slast_ref[r] == 1
    for h in range(GDN_HEADS):
        cols = slice(h * HEAD_DIM, (h + 1) * HEAD_DIM)
        x = cur_ref[:, cols].astype(F32)
        prev_row = jnp.where(is_first, 0.0, prev_ref[:, cols].astype(F32)[HALO - 1:HALO, :])
        next_row = jnp.where(is_last, 0.0, next_ref[:, cols].astype(F32)[0:1, :])
        xm = jnp.where(row == 0, prev_row, pltpu.roll(x, 1, 0))
        xp = jnp.where(row == n - 1, next_row, pltpu.roll(x, n - 1, 0))
        y = _silu(w_ref[0:1, cols] * xm + w_ref[1:2, cols] * x + w_ref[2:3, cols] * xp)
        rs = lax.rsqrt(jnp.sum(y * y, axis=-1, keepdims=True) + RMS_EPS)
        factor = jnp.where(part == 0, rs * (HEAD_DIM ** -0.5), jnp.where(part == 1, rs, 1.0))
        o_ref[:, cols] = y * factor


def _gdn_conv(z1, conv_w, seq_first, seq_last):
    M = z1.shape[0]
    nblk = M // CONV_ROWS
    W = GDN_HEADS * HEAD_DIM
    c0 = 3072 // W
    per = CONV_ROWS // HALO
    nh = M // HALO
    return pl.pallas_call(
        _gdn_conv_kernel,
        grid_spec=pltpu.PrefetchScalarGridSpec(
            num_scalar_prefetch=2, grid=(nblk, 3),
            in_specs=[
                pl.BlockSpec((HALO, W), lambda r, c, a, b: (jnp.maximum(r * per - 1, 0), c0 + c)),
                pl.BlockSpec((CONV_ROWS, W), lambda r, c, a, b: (r, c0 + c)),
                pl.BlockSpec((HALO, W), lambda r, c, a, b: (jnp.minimum((r + 1) * per, nh - 1), c0 + c)),
                pl.BlockSpec((3, W), lambda r, c, a, b: (0, c)),
            ],
            out_specs=pl.BlockSpec((CONV_ROWS, W), lambda r, c, a, b: (r, c))),
        out_shape=jax.ShapeDtypeStruct((M, 3 * W), F32),
        compiler_params=_cparams(("arbitrary", "arbitrary")),
        name="gdn_conv",
    )(seq_first, seq_last, z1, z1, z1, conv_w)


def _chunk_tri(n, upper):
    r = lax.broadcasted_iota(jnp.int32, (n, n), 0)
    c = lax.broadcasted_iota(jnp.int32, (n, n), 1)
    same = (r // CHUNK) == (c // CHUNK)
    tri = (c >= r) if upper else (c <= r)
    return jnp.where(same & tri, 1.0, 0.0).astype(BF16)


def _gdn_gates_kernel(zg_ref, a_ref, dtb_ref, o_ref):
    zg = zg_ref[...]
    n = zg.shape[0]
    lane = lax.broadcasted_iota(jnp.int32, zg.shape, 1)
    beta = _sigmoid(zg)
    t = zg + dtb_ref[...]
    softplus = jnp.maximum(t, 0.0) + jnp.log(1.0 + jnp.exp(-jnp.abs(t)))
    g = jnp.where((lane >= 16) & (lane < 32), a_ref[...] * softplus, 0.0)
    gc_f = _dot_exact_lhs(_chunk_tri(n, False), g)
    gc_b = _dot_exact_lhs(_chunk_tri(n, True), g)
    o_ref[...] = jnp.where(lane < 16, beta, jnp.where(lane < 24, gc_f, gc_b))


def _gdn_gates(zg, a_row, dtb_row):
    M = zg.shape[0]
    tm = 256
    return pl.pallas_call(
        _gdn_gates_kernel,
        grid=(M // tm,),
        in_specs=[pl.BlockSpec((tm, LANES), lambda i: (i, 0)),
                  pl.BlockSpec((1, LANES), lambda i: (0, 0)),
                  pl.BlockSpec((1, LANES), lambda i: (0, 0))],
        out_specs=pl.BlockSpec((tm, LANES), lambda i: (i, 0)),
        out_shape=jax.ShapeDtypeStruct((M, LANES), F32),
        compiler_params=_cparams(("arbitrary",)),
        name="gdn_gates",
    )(zg, a_row, dtb_row)


_INV_BASE = 8


def _unit_tri_inverse(lms, r, c):
    C = lms[0].shape[0]
    eye = jnp.where(r == c, 1.0, 0.0)
    diag = (r // _INV_BASE) == (c // _INV_BASE)
    dms = [jnp.where(diag, lm, 0.0) for lm in lms]
    d16 = [dm.astype(BF16) for dm in dms]
    p16 = [_dot(d, d).astype(BF16) for d in d16]
    xs = [eye - dm for dm in dms]
    rrs = [_dot(jnp.concatenate([p, x.astype(BF16)], axis=0), p) for p, x in zip(p16, xs)]
    xs = [x + rr[C:] for x, rr in zip(xs, rrs)]
    ts = [x + _dot(x.astype(BF16), rr[:C].astype(BF16)) for x, rr in zip(xs, rrs)]
    s = _INV_BASE
    while s < C:
        join = ((r // (2 * s)) == (c // (2 * s))) & ((r // s) != (c // s))
        e16 = [jnp.where(join, lm, 0.0).astype(BF16) for lm in lms]
        t16 = [t.astype(BF16) for t in ts]
        ys = [_dot(e, t).astype(BF16) for e, t in zip(e16, t16)]
        ts = [t - _dot(tb, y) for t, tb, y in zip(ts, t16, ys)]
        s *= 2
    return ts


GDN_PREP_CHUNKS = 16


def _gdn_prep_kernel(q_ref, k_ref, v_ref, g_ref, rf_ref, rb_ref, wq_ref, u_ref, qkkd_ref, gl_ref):
    C = CHUNK
    h = pl.program_id(1)
    r = lax.broadcasted_iota(jnp.int32, (C, C), 0)
    c = lax.broadcasted_iota(jnp.int32, (C, C), 1)
    lane = lax.broadcasted_iota(jnp.int32, (C, LANES), 1)
    row_refs = (rf_ref, rb_ref)
    qs, ks, vs, kks, qks, gts = [], [], [], [], [], []
    for n in range(GDN_PREP_CHUNKS):
        rows = slice(n * C, (n + 1) * C)
        qs.append(q_ref[rows, :])
        ks.append(k_ref[rows, :])
        vs.append(v_ref[rows, :])
        gts.append(g_ref[rows, :])
        kb16 = ks[n].astype(BF16)
        kks.append(_dot_nt(kb16, kb16))
        qks.append(_dot_nt(qs[n].astype(BF16), kb16))

    def column(tile, ch):
        return jnp.sum(jnp.where(lane == ch, tile, 0.0), axis=-1, keepdims=True)

    units = [(n, d) for n in range(GDN_PREP_CHUNKS) for d in range(2)]
    lms, rhs1 = [], []
    for n, d in units:
        beta = column(gts[n], d * GDN_HEADS + h)
        gc = column(gts[n], 16 + d * GDN_HEADS + h)
        grow = row_refs[d][0, n][0:1, :]
        incl = (c <= r) if d == 0 else (c >= r)
        strict = (c < r) if d == 0 else (c > r)
        last = C - 1 if d == 0 else 0
        decay = jnp.where(incl, jnp.exp(jnp.where(incl, gc - grow, 0.0)), 0.0)
        lms.append(jnp.where(strict, beta * kks[n] * decay, 0.0))
        eg = jnp.exp(gc)
        g_last = gc[last:last + 1, :]
        rhs1.append(jnp.concatenate([ks[n] * (beta * eg), vs[n] * beta], axis=-1).astype(BF16))
        wq_ref[d, (2 * n + 1) * C:(2 * n + 2) * C, :] = (qs[n] * eg).astype(BF16)
        kd = ks[n] * jnp.exp(g_last - gc)
        base = n * (C + HEAD_DIM)
        qkkd_ref[d, 0, base:base + C, :] = (qks[n] * decay).astype(BF16)
        qkkd_ref[d, 0, base + C:base + C + HEAD_DIM, :] = jnp.transpose(kd).astype(BF16)
        gl_ref[d, 0, n * 8:(n + 1) * 8, :] = jnp.broadcast_to(jnp.exp(g_last), (8, HEAD_DIM))
    t16 = [t.astype(BF16) for t in _unit_tri_inverse(lms, r, c)]
    wa = [_dot(t, r1) for t, r1 in zip(t16, rhs1)]
    for (n, d), wu in zip(units, wa):
        wq_ref[d, 2 * n * C:(2 * n + 1) * C, :] = wu[:, :HEAD_DIM].astype(BF16)
        u_ref[d, n * C:(n + 1) * C, :] = wu[:, HEAD_DIM:]


def _gdn_prep(qkv, gates, rowb):
    M = qkv.shape[0]
    H = GDN_HEADS
    N = GDN_PREP_CHUNKS
    C = CHUNK
    nch = M // (C * N)
    rb_spec = lambda ch0: pl.BlockSpec((1, N, 8, C), lambda c, h: (ch0 + h, c, 0, 0))
    return pl.pallas_call(
        _gdn_prep_kernel,
        grid=(nch, H),
        in_specs=[
            pl.BlockSpec((N * C, HEAD_DIM), lambda c, h: (c, h)),
            pl.BlockSpec((N * C, HEAD_DIM), lambda c, h: (c, H + h)),
            pl.BlockSpec((N * C, HEAD_DIM), lambda c, h: (c, 2 * H + h)),
            pl.BlockSpec((N * C, LANES), lambda c, h: (c, 0)),
            rb_spec(0), rb_spec(H),
        ],
        out_specs=[
            pl.BlockSpec((2, N * 2 * C, HEAD_DIM), lambda c, h: (0, c, h)),
            pl.BlockSpec((2, N * C, HEAD_DIM), lambda c, h: (0, c, h)),
            pl.BlockSpec((2, 1, N * (C + HEAD_DIM), C), lambda c, h: (0, h, c, 0)),
            pl.BlockSpec((2, 1, N * 8, HEAD_DIM), lambda c, h: (0, h, c, 0)),
        ],
        name="gdn_prep",
        out_shape=[
            jax.ShapeDtypeStruct((2, 2 * M, H * HEAD_DIM), BF16),
            jax.ShapeDtypeStruct((2, M, H * HEAD_DIM), F32),
            jax.ShapeDtypeStruct((2, H, (M // C) * (C + HEAD_DIM), C), BF16),
            jax.ShapeDtypeStruct((2, H, (M // C) * 8, HEAD_DIM), F32),
        ],
        compiler_params=_cparams(("arbitrary", "arbitrary")),
    )(qkv, qkv, qkv, gates, rowb, rowb)


def _gdn_state_kernel(wqf_ref, uf_ref, qkf_ref, glf_ref, wqb_ref, ub_ref, qkb_ref, glb_ref,
                      of_ref, ob_ref, s_ref):
    C = CHUNK
    H = GDN_HEADS

    @pl.when(pl.program_id(1) == 0)
    def _():
        s_ref[...] = jnp.zeros(s_ref.shape, F32)

    dirs = ((wqf_ref, uf_ref, qkf_ref, glf_ref, of_ref), (wqb_ref, ub_ref, qkb_ref, glb_ref, ob_ref))
    units = [(d, h) for d in range(2) for h in range(H)]
    states = [s_ref[d * H + h] for d, h in units]
    for step in range(SCAN_CHUNKS):
        subs = [step if d == 0 else SCAN_CHUNKS - 1 - step for d, _ in units]
        m1s = []
        for (d, h), sub, st in zip(units, subs, states):
            cols = slice(h * HEAD_DIM, (h + 1) * HEAD_DIM)
            m1s.append(_dot(dirs[d][0][0, sub * 2 * C:(sub + 1) * 2 * C, cols], st.astype(BF16)))
        m2s = []
        for (d, h), sub, m1 in zip(units, subs, m1s):
            cols = slice(h * HEAD_DIM, (h + 1) * HEAD_DIM)
            v_new = dirs[d][1][0, sub * C:(sub + 1) * C, cols] - m1[0:C]
            m2s.append(_dot(dirs[d][2][0, h, sub * (C + HEAD_DIM):(sub + 1) * (C + HEAD_DIM), :],
                            v_new.astype(BF16)))
        new_states = []
        for (d, h), sub, st, m1, m2 in zip(units, subs, states, m1s, m2s):
            cols = slice(h * HEAD_DIM, (h + 1) * HEAD_DIM)
            dirs[d][4][sub * C:(sub + 1) * C, cols] = (m1[C:2 * C] + m2[0:C]).astype(BF16)
            new_states.append(st * dirs[d][3][0, h, sub * 8:sub * 8 + 1, :] + m2[C:])
        states = new_states
    for (d, h), st in zip(units, states):
        s_ref[d * H + h] = st


SCAN_CHUNKS = 4


def _seq_chunk_maps(B, S, L):
    C = CHUNK * SCAN_CHUNKS
    lc, sc = L // C, S // C
    ctx0 = (B * S) // C

    def fwd(b, c):
        return jnp.where(c < lc, ctx0 + b * lc + c, b * sc + (c - lc))

    def bwd(b, c):
        return jnp.where(c < lc, ctx0 + b * lc + (lc - 1 - c), b * sc + (sc - 1 - (c - lc)))

    return fwd, bwd, lc + sc


def _gdn_state(wq, u, qkkd, gl, *, B, S, L):
    C = CHUNK
    H = GDN_HEADS
    M = u.shape[1]
    fwd, bwd, nc = _seq_chunk_maps(B, S, L)

    N = SCAN_CHUNKS

    def specs(d, pos):
        return [
            pl.BlockSpec((1, N * 2 * C, H * HEAD_DIM), lambda b, c: (d, pos(b, c), 0)),
            pl.BlockSpec((1, N * C, H * HEAD_DIM), lambda b, c: (d, pos(b, c), 0)),
            pl.BlockSpec((1, H, N * (C + HEAD_DIM), C), lambda b, c: (d, 0, pos(b, c), 0)),
            pl.BlockSpec((1, H, N * 8, HEAD_DIM), lambda b, c: (d, 0, pos(b, c), 0)),
        ]

    return pl.pallas_call(
        _gdn_state_kernel,
        grid=(B, nc),
        in_specs=specs(0, fwd) + specs(1, bwd),
        out_specs=[pl.BlockSpec((N * C, H * HEAD_DIM), lambda b, c: (fwd(b, c), 0)),
                   pl.BlockSpec((N * C, H * HEAD_DIM), lambda b, c: (bwd(b, c), 0))],
        out_shape=[jax.ShapeDtypeStruct((M, H * HEAD_DIM), BF16)] * 2,
        scratch_shapes=[pltpu.VMEM((2 * H, HEAD_DIM, HEAD_DIM), F32)],
        compiler_params=_cparams(("arbitrary", "arbitrary")),
        name="gdn_state",
    )(wq, u, qkkd, gl, wq, u, qkkd, gl)


def _gla_kernel(qf_ref, kf_ref, vf_ref, zf_ref, qb_ref, kb_ref, vb_ref, zb_ref, wgk_ref, bgk_ref,
                of_ref, ob_ref, s_ref):
    C = CHUNK
    r = lax.broadcasted_iota(jnp.int32, (C, C), 0)
    c = lax.broadcasted_iota(jnp.int32, (C, C), 1)

    @pl.when(pl.program_id(1) == 0)
    def _():
        s_ref[...] = jnp.zeros(s_ref.shape, F32)

    dirs = ((qf_ref, kf_ref, vf_ref, zf_ref, of_ref), (qb_ref, kb_ref, vb_ref, zb_ref, ob_ref))
    incls = [(c <= r), (c >= r)]
    nrows = SCAN_CHUNKS * C
    bcums = []
    for d in range(2):
        pre = _dot(dirs[d][3][...].astype(BF16), wgk_ref[d].astype(BF16)) + bgk_ref[d]
        gk = (jnp.minimum(pre, 0.0) - jnp.log(1.0 + jnp.exp(-jnp.abs(pre)))) * (1.0 / GLA_GATE_NORM)
        bcums.append(_dot_exact_lhs(_chunk_tri(nrows, d == 1), gk))
    units = [(d, h) for d in range(2) for h in range(GLA_HEADS)]
    states = [s_ref[d * GLA_HEADS + h] for d, h in units]
    for step in range(SCAN_CHUNKS):
        scores, inter, upd, dls = [], [], [], []
        for (d, h), st in zip(units, states):
            sub = step if d == 0 else SCAN_CHUNKS - 1 - step
            rows = slice(sub * C, (sub + 1) * C)
            kc = slice(h * GLA_DK, (h + 1) * GLA_DK)
            vc = slice(h * GLA_DV, (h + 1) * GLA_DV)
            mid = C // 2 - 1 if d == 1 else C // 2
            last = 0 if d == 1 else C - 1
            q = dirs[d][0][rows, kc].astype(F32) * (GLA_DK ** -0.5)
            k = dirs[d][1][rows, kc].astype(F32)
            b = bcums[d][sub * C:(sub + 1) * C, kc]
            b_mid = b[mid:mid + 1, :]
            b_last = b[last:last + 1, :]
            scores.append(_dot_nt((q * jnp.exp(b - b_mid)).astype(BF16), (k * jnp.exp(b_mid - b)).astype(BF16)))
            inter.append(_dot((q * jnp.exp(b)).astype(BF16), st.astype(BF16)))
            upd.append(_dot_tn((k * jnp.exp(b_last - b)).astype(BF16), dirs[d][2][rows, vc]))
            dls.append(jnp.transpose(jnp.broadcast_to(jnp.exp(b_last), (GLA_DK, GLA_DK))))
        new_states = []
        for (d, h), st, sc, oi, up, dl in zip(units, states, scores, inter, upd, dls):
            sub = step if d == 0 else SCAN_CHUNKS - 1 - step
            rows = slice(sub * C, (sub + 1) * C)
            vc = slice(h * GLA_DV, (h + 1) * GLA_DV)
            p = jnp.where(incls[d], sc, 0.0).astype(BF16)
            dirs[d][4][rows, vc] = (_dot(p, dirs[d][2][rows, vc]) + oi).astype(BF16)
            new_states.append(st * jnp.concatenate([dl, dl], axis=-1) + up)
        states = new_states
    for (d, h), st in zip(units, states):
        s_ref[d * GLA_HEADS + h] = st


def _gla(z2, zg, wgk_pad, bgk, *, B, S, L):
    C = CHUNK * SCAN_CHUNKS
    M = z2.shape[0]
    fwd, bwd, nc = _seq_chunk_maps(B, S, L)
    kw = GLA_HEADS * GLA_DK
    vw = GLA_HEADS * GLA_DV

    def specs(pos):
        return [
            pl.BlockSpec((C, kw), lambda b, c: (pos(b, c), 0)),
            pl.BlockSpec((C, kw), lambda b, c: (pos(b, c), 1)),
            pl.BlockSpec((C, vw), lambda b, c: (pos(b, c), 1)),
            pl.BlockSpec((C, LANES), lambda b, c: (pos(b, c), 0)),
        ]

    return pl.pallas_call(
        _gla_kernel,
        grid=(B, nc),
        in_specs=specs(fwd) + specs(bwd) + [
            pl.BlockSpec((2, LANES, kw), lambda b, c: (0, 0, 0)),
            pl.BlockSpec((2, 1, kw), lambda b, c: (0, 0, 0)),
        ],
        out_specs=[pl.BlockSpec((C, vw), lambda b, c: (fwd(b, c), 0)),
                   pl.BlockSpec((C, vw), lambda b, c: (bwd(b, c), 0))],
        out_shape=[jax.ShapeDtypeStruct((M, vw), BF16)] * 2,
        scratch_shapes=[pltpu.VMEM((2 * GLA_HEADS, GLA_DK, GLA_DV), F32)],
        compiler_params=_cparams(("arbitrary", "arbitrary")),
        name="gla",
    )(z2, z2, z2, zg, z2, z2, z2, zg, wgk_pad, bgk)


def _headnorm_kernel(of_ref, ob_ref, z_ref, g_ref, o_ref, *, hd):
    n = of_ref.shape[1] // hd
    for h in range(n):
        cols = slice(h * hd, (h + 1) * hd)
        o = of_ref[:, cols].astype(F32) + ob_ref[:, cols].astype(F32)
        y = o * lax.rsqrt(jnp.mean(o * o, axis=-1, keepdims=True) + RMS_EPS) * g_ref[...]
        o_ref[:, cols] = (y * _silu(z_ref[:, cols].astype(F32))).astype(o_ref.dtype)


def _headnorm(of, ob, z, zcol, gain, *, hd, tm=512):
    M, W = of.shape
    return pl.pallas_call(
        functools.partial(_headnorm_kernel, hd=hd),
        grid=(M // tm,),
        in_specs=[pl.BlockSpec((tm, W), lambda i: (i, 0)),
                  pl.BlockSpec((tm, W), lambda i: (i, 0)),
                  pl.BlockSpec((tm, W), lambda i: (i, zcol)),
                  pl.BlockSpec((1, hd), lambda i: (0, 0))],
        out_specs=pl.BlockSpec((tm, W), lambda i: (i, 0)),
        out_shape=jax.ShapeDtypeStruct((M, W), BF16),
        compiler_params=_cparams(("arbitrary",)),
        name="headnorm",
    )(of, ob, z, gain.reshape(1, hd))


def _final_kernel(rowid_ref, *refs, with_y, norm):
    del rowid_ref
    refs = list(refs)
    x = refs.pop(0)[...]
    if with_y:
        y0_ref, y1_ref, gt_ref = refs[:3]
        refs = refs[3:]
        x = x + gt_ref[0] * (y0_ref[...].astype(F32) + y1_ref[...].astype(F32))
    if norm:
        g_ref = refs.pop(0)
        x = x * lax.rsqrt(jnp.mean(x * x, axis=-1, keepdims=True) + RMS_EPS) * g_ref[...]
    refs[0][...] = x


def _final(X, ys, mod3, rowid, layer, k_gate, g_final, *, tm, grid_m):
    D = X.shape[1]
    row_spec = pl.BlockSpec((tm, D), lambda i, r: (i, 0))
    args, in_specs = [X], [row_spec]
    if ys is not None:
        args += [ys[0], ys[1], mod3]
        in_specs += [row_spec, row_spec,
                     pl.BlockSpec((1, 1, D), lambda i, r: (layer * 16 + r[i], 0, k_gate))]
    if g_final is not None:
        args.append(g_final.reshape(1, D))
        in_specs.append(pl.BlockSpec((1, D), lambda i, r: (0, 0)))
    return pl.pallas_call(
        functools.partial(_final_kernel, with_y=ys is not None, norm=g_final is not None),
        grid_spec=pltpu.PrefetchScalarGridSpec(
            num_scalar_prefetch=1, grid=(grid_m,), in_specs=in_specs, out_specs=row_spec),
        out_shape=jax.ShapeDtypeStruct((grid_m * tm, D), F32),
        compiler_params=_cparams(("arbitrary",)),
        name="final",
    )(rowid, *args)


def _rope_tables(S):
    rows = S // GRID_W
    row = jnp.repeat(jnp.arange(rows, dtype=F32), GRID_W)
    col = jnp.tile(jnp.arange(GRID_W, dtype=F32), rows)
    inv_freq = ROPE_THETA ** (-jnp.arange(ROPE_FREQS, dtype=F32) / ROPE_FREQS)
    ar = row[:, None] * inv_freq
    ac = col[:, None] * inv_freq
    cos = jnp.concatenate([jnp.cos(ar), jnp.cos(ar), jnp.cos(ac), jnp.cos(ac)], axis=-1)
    sin = jnp.concatenate([-jnp.sin(ar), jnp.sin(ar), -jnp.sin(ac), jnp.sin(ac)], axis=-1)
    return cos, sin


def _stream_rowid(B, S, L, tm, grid_m):
    i = jnp.arange(grid_m, dtype=jnp.int32) * tm
    return jnp.where(i < B * S, i // S, B).astype(jnp.int32)


def _moe_plan(logits, T, tm, n_tiles):
    top_val, top_idx = lax.top_k(logits, TOP_K)
    top_w = jax.nn.softmax(top_val, axis=-1)
    e_flat = top_idx.reshape(-1).astype(jnp.int32)
    w_flat = top_w.reshape(-1)
    n_assign = T * TOP_K
    order = jnp.argsort(e_flat, stable=True).astype(jnp.int32)
    inv = jnp.argsort(order).astype(jnp.int32)
    experts = jnp.arange(N_EXPERTS, dtype=jnp.int32)
    sizes = jnp.sum((e_flat[:, None] == experts[None, :]).astype(jnp.int32), axis=0)
    start = jnp.cumsum(sizes) - sizes
    padded = ((sizes + tm - 1) // tm) * tm
    pend = jnp.cumsum(padded)
    pstart = pend - padded
    dest = pstart[e_flat] + inv - start[e_flat]
    tile0 = jnp.arange(n_tiles, dtype=jnp.int32) * tm
    tile_e = jnp.minimum(jnp.searchsorted(pend, tile0, side='right'), N_EXPERTS - 1).astype(jnp.int32)
    valid = (tile0 < pend[-1]).astype(jnp.int32)
    row = jnp.arange(n_tiles * tm, dtype=jnp.int32)
    row_e = jnp.repeat(tile_e, tm)
    row_rank = row - pstart[row_e]
    row_ok = (row_rank < sizes[row_e]) & (jnp.repeat(valid, tm) == 1)
    row_a = order[jnp.clip(start[row_e] + row_rank, 0, n_assign - 1)]
    row_token = jnp.where(row_ok, row_a // TOP_K, row % T).astype(jnp.int32)
    row_w = jnp.where(row_ok, w_flat[row_a], 0.0)
    last_e = tile_e[jnp.maximum(pend[-1] // tm - 1, 0)]
    tile_e = jnp.where(valid == 1, tile_e, last_e)
    first = jnp.concatenate([jnp.ones((1,), jnp.int32), (tile_e[1:] != tile_e[:-1]).astype(jnp.int32)])
    return row_token, row_w, dest.reshape(T, TOP_K), tile_e, first, valid


def kernel(x, c, ctx, c_ctx, w_ada, b_ada, g_norm_mix, g_norm_ffn, w_in, a_q_gain, a_k_gain, b_sink,
           gdn_conv, gdn_a_log, gdn_dt_bias, gdn_norm_gain, gla_w_gk, gla_b_gk, gla_norm_gain,
           w_branch, w_merge, w_out, w1_dense, w3_dense, w2_dense, w_router, w1_moe, w3_moe, w2_moe,
           g_final):
    B, S, D = x.shape
    L = ctx.shape[1]
    depth = w_in.shape[0]
    n_lat = B * S
    M = n_lat + B * L
    TM = 512
    TM_BIG = 1024
    assert M % (CHUNK * GDN_PREP_CHUNKS) == 0 and L % (CHUNK * SCAN_CHUNKS) == 0
    assert D == D_MODEL and S % CONV_ROWS == 0 and L % CONV_ROWS == 0 and n_lat % TM == 0
    assert (B * L) % TM == 0 and S % TM == 0 and S >= 3 * Q_BLOCK and n_lat % L == 0 and B <= 15

    cond = jnp.zeros((16, D), F32).at[:B].set(c).at[B].set(c_ctx)
    cond2 = jnp.concatenate([cond] * depth, axis=0)
    (mod,) = _panel_call(
        _ep_ada, name="ada", grid_n=(6 * D) // 512, grid_m=depth,
        lhs=[(cond2, (16, D), lambda j, i, e, f, v, r: (i, 0))],
        weights=[(w_ada, (1, D, 512), lambda j, i, e, f, v, r: (e[i], 0, j))],
        extras=[(b_ada.reshape(depth, 1, 6 * D), (1, 1, 512), lambda j, i, e, f, v, r: (e[i], 0, j))],
        outs=[(jax.ShapeDtypeStruct((16 * depth, 6 * D), F32), (16, 512), lambda j, i, e, f, v, r: (i, j))],
        eid=jnp.arange(depth, dtype=jnp.int32), first=jnp.ones((depth,), jnp.int32))
    mod3 = mod.reshape(16 * depth, 1, 6 * D)

    cos, sin = _rope_tables(S)
    w_in_t = jnp.swapaxes(w_in, 1, 2)
    X = jnp.concatenate([x.reshape(n_lat, D), ctx.reshape(B * L, D)], axis=0)
    gm_all = M // TM
    gm_lat = n_lat // TM
    rowid = _stream_rowid(B, S, L, TM, gm_all)

    blk = jnp.arange(M // CONV_ROWS, dtype=jnp.int32) * CONV_ROWS
    seq_len = jnp.where(blk < n_lat, S, L)
    seq_off = jnp.where(blk < n_lat, blk, blk - n_lat)
    seq_first = (seq_off % seq_len == 0).astype(jnp.int32)
    seq_last = ((seq_off + CONV_ROWS) % seq_len == 0).astype(jnp.int32)

    out = None
    for layer in range(depth):
        last = layer == depth - 1
        gm = gm_lat if last else gm_all
        h = _modulate(X, g_norm_mix, mod3, rowid, layer, 0, 1, tm=TM, grid_m=M // TM)
        tm_z = TM_BIG if M % TM_BIG == 0 else TM
        z1 = _matmul_stream(h, w_in_t, layer, 0, Z1_WIDTH, name="proj_z1", tm=tm_z, tn=1024, out_dtype=BF16)
        w_d = lax.slice(w_in_t, (layer, COL_D_Q, 0), (layer + 1, COL_D_GATE, D)).reshape(Z2_WIDTH, D)
        z2 = _matmul_stream(h, w_d, layer, 0, Z2_WIDTH, name="proj_z2", tm=tm_z, tn=1024, out_dtype=BF16)
        w_g = jnp.concatenate([
            lax.slice(w_in_t, (layer, COL_C_BETA, 0), (layer + 1, COL_D_Q, D)).reshape(COL_D_Q - COL_C_BETA, D),
            lax.slice(w_in_t, (layer, COL_D_GATE, 0), (layer + 1, PROJ_WIDTH, D)).reshape(PROJ_WIDTH - COL_D_GATE, D),
            jnp.zeros((LANES - 64, D), F32)], axis=0)
        zg = _matmul_stream(h, w_g, layer, 0, LANES, name="proj_gates", tm=TM, tn=LANES, out_dtype=F32)

        oa = _attention(z1, cos, sin, a_q_gain[layer], a_k_gain[layer], None, mode='all',
                        B=B, S=S, L=L, qcol=0, kcol=8, vcol=10)
        ob = _attention(z1, cos, sin, None, None, b_sink[layer], mode='band',
                        B=B, S=S, L=L, qcol=3, kcol=20, vcol=22)
        if not last:
            oa_c = _attention(z1, cos, sin, a_q_gain[layer], a_k_gain[layer], None, mode='ctx',
                              B=B, S=S, L=L, qcol=0, kcol=8, vcol=10)
            ob_c = _attention(z1, cos, sin, None, None, b_sink[layer], mode='ctx',
                              B=B, S=S, L=L, qcol=3, kcol=20, vcol=22)
            oa = jnp.concatenate([oa, oa_c], axis=0)
            ob = jnp.concatenate([ob, ob_c], axis=0)

        qkv = _gdn_conv(z1, gdn_conv[layer], seq_first, seq_last)
        a_row = jnp.zeros((1, LANES), F32).at[0, 16:32].set(-jnp.exp(gdn_a_log[layer]).reshape(-1))
        dtb_row = jnp.zeros((1, LANES), F32).at[0, 16:32].set(gdn_dt_bias[layer].reshape(-1))
        gates = _gdn_gates(zg, a_row, dtb_row)
        gt = gates[:, 16:32].T
        rowb = jnp.broadcast_to(gt.reshape(16, M // CHUNK, 1, CHUNK), (16, M // CHUNK, 8, CHUNK))
        wq, u, qkkd, gl = _gdn_prep(qkv, gates, rowb)
        oc_f, oc_b = _gdn_state(wq, u, qkkd, gl, B=B, S=S, L=L)
        oc = _headnorm(oc_f, oc_b, z1, 6144 // BRANCH_WIDTH, gdn_norm_gain[layer], hd=HEAD_DIM)

        wgk_pad = jnp.zeros((2, LANES, GLA_HEADS * GLA_DK), F32)
        wgk_pad = wgk_pad.at[0, 32:48].set(gla_w_gk[layer, 0]).at[1, 48:64].set(gla_w_gk[layer, 1])
        od_f, od_b = _gla(z2, zg, wgk_pad, gla_b_gk[layer].reshape(2, 1, -1), B=B, S=S, L=L)
        od = _headnorm(od_f, od_b, z2, 2048 // BRANCH_WIDTH, gla_norm_gain[layer], hd=GLA_DV)

        tn_m = 256
        branches = [oa, ob, oc, od]
        lhs = [(h, (TM, D), lambda j, i, e, f, v, r: (i, 0))]
        lhs += [(o, (TM, BRANCH_WIDTH), lambda j, i, e, f, v, r: (i, 0)) for o in branches]
        wts = [(w_merge, (1, D, tn_m), lambda j, i, e, f, v, r, n=n: (layer, 0, n * (D // tn_m) + j))
               for n in range(N_BRANCH)]
        wts += [(w_branch, (1, 1, BRANCH_WIDTH, tn_m), lambda j, i, e, f, v, r, n=n: (layer, n, 0, j))
                for n in range(N_BRANCH)]
        (merged,) = _panel_call(
            _ep_merge, name="merge", grid_n=D // tn_m, grid_m=gm, lhs=lhs, weights=wts, extras=[],
            outs=[(jax.ShapeDtypeStruct((gm * TM, D), BF16), (TM, tn_m), lambda j, i, e, f, v, r: (i, j))])

        def resid(name, xin, w, wmap, kdim, k_gate, tn):
            (res,) = _panel_call(
                _ep_resid, name=name, grid_n=D // tn, grid_m=gm,
                lhs=[(xin, (TM, kdim), lambda j, i, e, f, v, r: (i, 0))],
                weights=[(w, (1, kdim, tn), wmap)],
                extras=[(X, (TM, tn), lambda j, i, e, f, v, r: (i, j)),
                        (mod3, (1, 1, tn), lambda j, i, e, f, v, r: (layer * 16 + r[i], 0, k_gate * (D // tn) + j))],
                outs=[(jax.ShapeDtypeStruct((gm * TM, D), F32), (TM, tn), lambda j, i, e, f, v, r: (i, j))],
                rowid=rowid[:gm])
            return res

        X = resid("out_proj", merged, w_out, lambda j, i, e, f, v, r: (layer, 0, j), D, 2, 1024)

        if layer % 2 == 0:
            li = layer // 2
            h2 = _modulate(X, g_norm_ffn, mod3, rowid, layer, 3, 4, tm=TM, grid_m=gm)
            d_ff = w1_dense.shape[2]
            tm_u = TM_BIG if (gm * TM) % TM_BIG == 0 else TM
            (uu,) = _panel_call(
                _ep_swiglu, name="dense_up", grid_n=d_ff // 512, grid_m=(gm * TM) // tm_u,
                lhs=[(h2, (tm_u, D), lambda j, i, e, f, v, r: (i, 0))],
                weights=[(w1_dense, (1, D, 512), lambda j, i, e, f, v, r: (li, 0, j)),
                         (w3_dense, (1, D, 512), lambda j, i, e, f, v, r: (li, 0, j))],
                extras=[],
                outs=[(jax.ShapeDtypeStruct((gm * TM, d_ff), BF16), (tm_u, 512), lambda j, i, e, f, v, r: (i, j))])
            X = resid("dense_down", uu, w2_dense, lambda j, i, e, f, v, r: (li, 0, j), d_ff, 5, 512)
            if last:
                out = _final(X, None, mod3, rowid, layer, 5, g_final, tm=TM, grid_m=n_lat // TM)
        else:
            li = layer // 2
            T = gm * TM
            wr_pad = jnp.zeros((D, LANES), F32).at[:, :N_EXPERTS].set(w_router[li])
            h2, logits = _modulate(X, g_norm_ffn, mod3, rowid, layer, 3, 4, tm=TM, grid_m=gm, w_router=wr_pad)
            tm_e = 512
            n_tiles = (T * TOP_K) // tm_e + N_EXPERTS
            row_token, row_w, dest, tile_e, first, valid = _moe_plan(logits[:, :N_EXPERTS], T, tm_e, n_tiles)
            xg = h2.at[row_token].get(mode="promise_in_bounds")
            d_ffe = w1_moe.shape[3]
            (uu,) = _panel_call(
                _ep_swiglu, name="moe_up", grid_n=d_ffe // 1024, grid_m=n_tiles,
                lhs=[(xg, (tm_e, D), lambda j, i, e, f, v, r: (i, 0))],
                weights=[(w1_moe, (1, 1, D, 1024), lambda j, i, e, f, v, r: (li, e[i], 0, j)),
                         (w3_moe, (1, 1, D, 1024), lambda j, i, e, f, v, r: (li, e[i], 0, j))],
                extras=[],
                outs=[(jax.ShapeDtypeStruct((n_tiles * tm_e, d_ffe), BF16), (tm_e, 1024),
                       lambda j, i, e, f, v, r: (i, j))],
                eid=tile_e, first=first, valid=valid)
            tm_d = 512
            tn_d = 512
            rep = tm_e // tm_d
            first_d = jnp.repeat(first, rep) * (jnp.arange(n_tiles * rep, dtype=jnp.int32) % rep == 0)
            (yo,) = _panel_call(
                _ep_rowscale, name="moe_down", grid_n=D // tn_d, grid_m=n_tiles * rep,
                lhs=[(uu, (tm_d, d_ffe), lambda j, i, e, f, v, r: (i, 0))],
                weights=[(w2_moe, (1, 1, d_ffe, tn_d), lambda j, i, e, f, v, r: (li, e[i], 0, j))],
                extras=[(row_w.reshape(-1, 1), (tm_d, 1), lambda j, i, e, f, v, r: (i, 0))],
                outs=[(jax.ShapeDtypeStruct((n_tiles * tm_e, D), BF16), (tm_d, tn_d),
                       lambda j, i, e, f, v, r: (i, j))],
                eid=jnp.repeat(tile_e, rep), first=first_d.astype(jnp.int32), valid=jnp.repeat(valid, rep))
            ys = (yo.at[dest[:, 0]].get(mode="promise_in_bounds"),
                  yo.at[dest[:, 1]].get(mode="promise_in_bounds"))
            res = _final(X, ys, mod3, rowid, layer, 5, g_final if last else None, tm=TM, grid_m=gm)
            if last:
                out = res
            else:
                X = res
    return out.reshape(B, S, D)
```
